```python
import math
import jax, jax.numpy as jnp
from jax import lax
import numpy as np

D_MODEL = 2048
BATCH = 16
SEQ = 2048
DEPTH = 4

HEAD_DIM = 128
FOX_HEADS = 6
DIFF_HEADS = 4
DIFF_QK_DIM = HEAD_DIM // 2
NSA_HEADS = 6
NSA_KV_GROUPS = 2
NSA_HPG = NSA_HEADS // NSA_KV_GROUPS
CMP_LEN = 32
CMP_STRIDE = 16
SEL_LEN = 64
SEL_TOPN = 8
WINDOW = 512
Q_BLOCK = 128
N_BUCKETS = 32
MAX_DISTANCE = 128
D_FF = 5632
CONV_W = 3
EPS = 1e-6
NEG_INF = -1e30
FORCE_SCORE = 1e4

IN_SIZES = (
    FOX_HEADS * HEAD_DIM, FOX_HEADS * HEAD_DIM, FOX_HEADS * HEAD_DIM, FOX_HEADS,
    DIFF_HEADS * 2 * DIFF_QK_DIM, DIFF_HEADS * 2 * DIFF_QK_DIM, DIFF_HEADS * HEAD_DIM,
    NSA_HEADS * HEAD_DIM,
    NSA_KV_GROUPS * HEAD_DIM, NSA_KV_GROUPS * HEAD_DIM,
    NSA_KV_GROUPS * HEAD_DIM, NSA_KV_GROUPS * HEAD_DIM,
    NSA_KV_GROUPS * HEAD_DIM, NSA_KV_GROUPS * HEAD_DIM,
    NSA_HEADS * 3,
)
N_IN = sum(IN_SIZES)

kernel_name = "hybrid_fox_diff_nsa_convffn_trunk"


def rmsnorm(x, g):
    xf = x.astype(jnp.float32)
    y = xf * lax.rsqrt(jnp.mean(xf * xf, axis=-1, keepdims=True) + EPS)
    return (y * g.astype(jnp.float32)).astype(x.dtype)


def masked_softmax(scores, mask):
    s = jnp.where(mask, scores.astype(jnp.float32), NEG_INF)
    return jax.nn.softmax(s, axis=-1)


def t5_bucket(rel):
    n = jnp.maximum(rel, 0)
    max_exact = N_BUCKETS // 2
    nf = jnp.maximum(n, 1).astype(jnp.float32)
    large = max_exact + (jnp.log(nf / max_exact) / math.log(MAX_DISTANCE / max_exact)
                         * (N_BUCKETS - max_exact)).astype(jnp.int32)
    large = jnp.minimum(large, N_BUCKETS - 1)
    return jnp.where(n < max_exact, n, large)


def to_blocks(a):
    b, s = a.shape[:2]
    a = a.reshape((b, s // Q_BLOCK, Q_BLOCK) + a.shape[2:])
    return jnp.moveaxis(a, 1, 0)


def from_blocks(a):
    a = jnp.moveaxis(a, 0, 1)
    return a.reshape((a.shape[0], a.shape[1] * a.shape[2]) + a.shape[3:])


def fox_attention(q, k, v, f_logit, f_bias):
    b, s, h, dh = q.shape
    log_f = jax.nn.log_sigmoid(f_logit.astype(jnp.float32) + f_bias.astype(jnp.float32))
    cum = jnp.cumsum(log_f, axis=1)
    cum_k = jnp.transpose(cum, (0, 2, 1))
    k_pos = jnp.arange(s)
    scale = dh ** -0.5

    def block(args):
        qb, cb, i = args
        q_pos = i * Q_BLOCK + jnp.arange(Q_BLOCK)
        sc = jnp.einsum('bqhd,bkhd->bhqk', qb, k).astype(jnp.float32) * scale
        sc = sc + jnp.transpose(cb, (0, 2, 1))[..., None] - cum_k[:, :, None, :]
        p = masked_softmax(sc, k_pos[None, :] <= q_pos[:, None])
        return jnp.einsum('bhqk,bkhd->bqhd', p.astype(v.dtype), v)

    out = lax.map(block, (to_blocks(q), to_blocks(cum), jnp.arange(s // Q_BLOCK)))
    return from_blocks(out).reshape(b, s, h * dh)


def diff_attention(q, k, v, rel_table, lam, lam_init, subln_g):
    b, s, h, _, dqk = q.shape
    k_pos = jnp.arange(s)
    scale = dqk ** -0.5

    def block(args):
        qb, i = args
        q_pos = i * Q_BLOCK + jnp.arange(Q_BLOCK)
        bias = jnp.moveaxis(rel_table[t5_bucket(q_pos[:, None] - k_pos[None, :])], -1, 0).astype(jnp.float32)
        sc = jnp.einsum('bqhmd,bkhmd->mbhqk', qb, k).astype(jnp.float32) * scale + bias
        p = masked_softmax(sc, k_pos[None, :] <= q_pos[:, None])
        attn = p[0] - lam * p[1]
        return jnp.einsum('bhqk,bkhd->bqhd', attn.astype(v.dtype), v)

    out = from_blocks(lax.map(block, (to_blocks(q), jnp.arange(s // Q_BLOCK))))
    out = rmsnorm(out, subln_g) * (1.0 - lam_init)
    return out.reshape(b, s, h * v.shape[-1])


def nsa_attention(q, kc, vc, ks, vs, kw, vw, gate_logit, rel_table, cmp_pos, wk1, wk2, wv1, wv2):
    b, s, g, hpg, dh = q.shape
    scale = dh ** -0.5
    t_pos = jnp.arange(s)

    n_cmp = (s - CMP_LEN) // CMP_STRIDE + 1
    n_sel = s // SEL_LEN
    n_top = min(SEL_TOPN, n_sel)
    cmp_starts = np.arange(n_cmp) * CMP_STRIDE
    sel_starts = np.arange(n_sel) * SEL_LEN
    tok = cmp_starts[:, None] + np.arange(CMP_LEN)

    def compress(a, w1, w2):
        blocks = a[:, tok] + cmp_pos[None, None, :, None, :]
        flat = jnp.transpose(blocks, (0, 1, 3, 2, 4)).reshape(b, n_cmp, g, CMP_LEN * dh)
        return jax.nn.gelu(flat @ w1) @ w2

    k_cmp = compress(kc, wk1, wk2)
    v_cmp = compress(vc, wv1, wv2)
    block_end = jnp.asarray(cmp_starts + CMP_LEN - 1)
    sc = jnp.einsum('bsghd,bngd->bghsn', q, k_cmp).astype(jnp.float32) * scale
    sc = sc + jnp.transpose(rel_table[t5_bucket(t_pos[:, None] - block_end[None, :])], (2, 3, 0, 1)).astype(jnp.float32)
    cmp_mask = block_end[None, :] <= t_pos[:, None]
    p_cmp = masked_softmax(sc, cmp_mask) * jnp.any(cmp_mask, axis=-1)[:, None]
    o_cmp = jnp.einsum('bghsn,bngd->bsghd', p_cmp.astype(vc.dtype), v_cmp)

    overlap = np.clip(np.minimum(cmp_starts[:, None] + CMP_LEN, sel_starts[None, :] + SEL_LEN)
                      - np.maximum(cmp_starts[:, None], sel_starts[None, :]), 0, None).astype(np.float32) / CMP_LEN
    imp = jnp.einsum('bghsn,nm->bgsm', p_cmp, jnp.asarray(overlap))
    blk_t = t_pos // SEL_LEN
    j = jnp.arange(n_sel)
    valid = j[None, :] <= blk_t[:, None]
    forced = (j[None, :] == 0) | (j[None, :] == blk_t[:, None]) | (j[None, :] == blk_t[:, None] - 1)
    score = jnp.where(valid, imp + jnp.where(forced, FORCE_SCORE, 0.0), NEG_INF)
    top_score, sel_idx = lax.top_k(score, n_top)
    sel_valid = top_score > 0.5 * NEG_INF
    sel_idx = jnp.transpose(sel_idx, (0, 2, 1, 3))
    sel_valid = jnp.transpose(sel_valid, (0, 2, 1, 3))

    ks_blk = jnp.transpose(ks.reshape(b, n_sel, SEL_LEN, g, dh), (0, 3, 1, 2, 4))
    vs_blk = jnp.transpose(vs.reshape(b, n_sel, SEL_LEN, g, dh), (0, 3, 1, 2, 4))
    kw_pad = jnp.pad(kw, ((0, 0), (WINDOW, 0), (0, 0), (0, 0)))
    vw_pad = jnp.pad(vw, ((0, 0), (WINDOW, 0), (0, 0), (0, 0)))
    rel_g = jnp.transpose(rel_table, (1, 0, 2))
    bi = jnp.arange(b)[:, None, None, None]
    gi = jnp.arange(g)[None, :, None, None]

    def block(args):
        qb, idxb, validb, i = args
        q_pos = i * Q_BLOCK + jnp.arange(Q_BLOCK)
        idx = jnp.transpose(idxb, (0, 2, 1, 3))
        kg = ks_blk[bi, gi, idx]
        vg = vs_blk[bi, gi, idx]
        tok_pos = idx[..., None] * SEL_LEN + jnp.arange(SEL_LEN)
        bias = rel_g[gi[..., None], t5_bucket(q_pos[:, None, None] - tok_pos)]
        scs = jnp.einsum('bqghd,bgqnkd->bghqnk', qb, kg).astype(jnp.float32) * scale \
            + jnp.moveaxis(bias, -1, 2).astype(jnp.float32)
        smask = jnp.transpose(validb, (0, 2, 1, 3))[..., None] & (tok_pos <= q_pos[:, None, None])
        ps = masked_softmax(scs.reshape(b, g, hpg, Q_BLOCK, n_top * SEL_LEN),
                            smask[:, :, None].reshape(b, g, 1, Q_BLOCK, n_top * SEL_LEN))
        o_sel = jnp.einsum('bghqk,bgqkd->bqghd', ps.astype(vs.dtype),
                           vg.reshape(b, g, Q_BLOCK, n_top * SEL_LEN, dh))
        kwb = lax.dynamic_slice_in_dim(kw_pad, i * Q_BLOCK, WINDOW + Q_BLOCK, axis=1)
        vwb = lax.dynamic_slice_in_dim(vw_pad, i * Q_BLOCK, WINDOW + Q_BLOCK, axis=1)
        k_pos = i * Q_BLOCK - WINDOW + jnp.arange(WINDOW + Q_BLOCK)
        rel = q_pos[:, None] - k_pos[None, :]
        wmask = (rel >= 0) & (rel < WINDOW) & (k_pos[None, :] >= 0)
        wbias = jnp.transpose(rel_table[t5_bucket(rel)], (2, 3, 0, 1)).astype(jnp.float32)
        scw = jnp.einsum('bqghd,bkgd->bghqk', qb, kwb).astype(jnp.float32) * scale + wbias
        pw = masked_softmax(scw, wmask)
        o_win = jnp.einsum('bghqk,bkgd->bqghd', pw.astype(vw.dtype), vwb)
        return o_sel, o_win

    o_sel, o_win = lax.map(block, (to_blocks(q), to_blocks(sel_idx), to_blocks(sel_valid), jnp.arange(s // Q_BLOCK)))
    o_sel = from_blocks(o_sel)
    o_win = from_blocks(o_win)
    gates = jax.nn.sigmoid(gate_logit.astype(jnp.float32)).astype(q.dtype)
    out = gates[..., 0:1] * o_cmp + gates[..., 1:2] * o_sel + gates[..., 2:3] * o_win
    return out.reshape(b, s, g * hpg * dh)


def conv_ffn(u, w_up, conv_w, conv_b, w_down):
    s = u.shape[1]
    hdn = u @ w_up
    hp = jnp.pad(hdn, ((0, 0), (CONV_W - 1, 0), (0, 0)))
    hdn = sum(hp[:, tap:tap + s] * conv_w[tap] for tap in range(CONV_W)) + conv_b
    gate, up = jnp.split(hdn, 2, axis=-1)
    return (jax.nn.silu(gate) * up) @ w_down


def setup_inputs(seed: int = 0) -> dict:
    key = jax.random.key(seed)
    ks = jax.random.split(key, 24)
    f32 = jnp.float32
    nrm = lambda k, shape, scale: jax.random.normal(k, shape, f32) * scale
    return {
        "x": jax.random.normal(ks[0], (BATCH, SEQ, D_MODEL), f32),
        "attn_norm_g": 1.0 + nrm(ks[1], (DEPTH, D_MODEL), 0.02),
        "w_in": nrm(ks[2], (DEPTH, D_MODEL, N_IN), D_MODEL ** -0.5),
        "fox_f_bias": jax.random.uniform(ks[3], (DEPTH, FOX_HEADS), f32, 1.0, 4.0),
        "diff_lq1": nrm(ks[4], (DEPTH, DIFF_QK_DIM), 0.1),
        "diff_lk1": nrm(ks[5], (DEPTH, DIFF_QK_DIM), 0.1),
        "diff_lq2": nrm(ks[6], (DEPTH, DIFF_QK_DIM), 0.1),
        "diff_lk2": nrm(ks[7], (DEPTH, DIFF_QK_DIM), 0.1),
        "diff_subln_g": 1.0 + nrm(ks[8], (DEPTH, 2 * DIFF_QK_DIM), 0.02),
        "nsa_cmp_pos": nrm(ks[9], (DEPTH, CMP_LEN, HEAD_DIM), 0.02),
        "nsa_cmp_wk1": nrm(ks[10], (DEPTH, CMP_LEN * HEAD_DIM, HEAD_DIM), (CMP_LEN * HEAD_DIM) ** -0.5),
        "nsa_cmp_wk2": nrm(ks[11], (DEPTH, HEAD_DIM, HEAD_DIM), HEAD_DIM ** -0.5),
        "nsa_cmp_wv1": nrm(ks[12], (DEPTH, CMP_LEN * HEAD_DIM, HEAD_DIM), (CMP_LEN * HEAD_DIM) ** -0.5),
        "nsa_cmp_wv2": nrm(ks[13], (DEPTH, HEAD_DIM, HEAD_DIM), HEAD_DIM ** -0.5),
        "w_out": nrm(ks[14], (DEPTH, D_MODEL, D_MODEL), D_MODEL ** -0.5),
        "ffn_norm_g": 1.0 + nrm(ks[15], (DEPTH, D_MODEL), 0.02),
        "ffn_w_up": nrm(ks[16], (DEPTH, D_MODEL, 2 * D_FF), D_MODEL ** -0.5),
        "ffn_conv_w": nrm(ks[17], (DEPTH, CONV_W, 2 * D_FF), CONV_W ** -0.5),
        "ffn_conv_b": nrm(ks[18], (DEPTH, 2 * D_FF), 0.01),
        "ffn_w_down": nrm(ks[19], (DEPTH, D_FF, D_MODEL), D_FF ** -0.5),
        "rel_bias": nrm(ks[20], (N_BUCKETS, DIFF_HEADS + NSA_HEADS), 0.2),
        "final_norm_g": 1.0 + nrm(ks[21], (D_MODEL,), 0.02),
    }


def reference(x, attn_norm_g, w_in, fox_f_bias, diff_lq1, diff_lk1, diff_lq2, diff_lk2, diff_subln_g,
              nsa_cmp_pos, nsa_cmp_wk1, nsa_cmp_wk2, nsa_cmp_wv1, nsa_cmp_wv2, w_out, ffn_norm_g,
              ffn_w_up, ffn_conv_w, ffn_conv_b, ffn_w_down, rel_bias, final_norm_g):
    b, s, _ = x.shape
    split_points = np.cumsum(IN_SIZES)[:-1].tolist()
    diff_table = rel_bias[:, :DIFF_HEADS]
    nsa_table = rel_bias[:, DIFF_HEADS:].reshape(N_BUCKETS, NSA_KV_GROUPS, NSA_HPG)
    G, HPG, Dh = NSA_KV_GROUPS, NSA_HPG, HEAD_DIM
    h = x
    for l in range(DEPTH):
        u = rmsnorm(h, attn_norm_g[l])
        proj = u @ w_in[l]
        (fq, fk, fv, ff, dq, dk, dv, nq, nkc, nvc, nks, nvs, nkw, nvw, ng) = jnp.split(proj, split_points, axis=-1)

        fox_o = fox_attention(fq.reshape(b, s, FOX_HEADS, Dh), fk.reshape(b, s, FOX_HEADS, Dh),
                              fv.reshape(b, s, FOX_HEADS, Dh), ff, fox_f_bias[l])

        lam_init = 0.8 - 0.6 * math.exp(-0.3 * l)
        lam = (jnp.exp(jnp.sum(diff_lq1[l].astype(jnp.float32) * diff_lk1[l].astype(jnp.float32)))
               - jnp.exp(jnp.sum(diff_lq2[l].astype(jnp.float32) * diff_lk2[l].astype(jnp.float32))) + lam_init)
        diff_o = diff_attention(dq.reshape(b, s, DIFF_HEADS, 2, DIFF_QK_DIM), dk.reshape(b, s, DIFF_HEADS, 2, DIFF_QK_DIM),
                                dv.reshape(b, s, DIFF_HEADS, Dh), diff_table, lam, lam_init, diff_subln_g[l])

        nsa_o = nsa_attention(nq.reshape(b, s, G, HPG, Dh),
                              nkc.reshape(b, s, G, Dh), nvc.reshape(b, s, G, Dh),
                              nks.reshape(b, s, G, Dh), nvs.reshape(b, s, G, Dh),
                              nkw.reshape(b, s, G, Dh), nvw.reshape(b, s, G, Dh),
                              ng.reshape(b, s, G, HPG, 3), nsa_table, nsa_cmp_pos[l],
                              nsa_cmp_wk1[l], nsa_cmp_wk2[l], nsa_cmp_wv1[l], nsa_cmp_wv2[l])

        mixed = jnp.concatenate([fox_o, diff_o, nsa_o], axis=-1)
        h = h + mixed @ w_out[l]
        h = h + conv_ffn(rmsnorm(h, ffn_norm_g[l]), ffn_w_up[l], ffn_conv_w[l], ffn_conv_b[l], ffn_w_down[l])
    return rmsnorm(h, final_norm_g)
```

```python
import functools
import math

import jax
import jax.numpy as jnp
import numpy as np
from jax import lax
from jax.experimental import pallas as pl
from jax.experimental.pallas import tpu as pltpu

D_MODEL = 2048
HEAD_DIM = 128
FOX_HEADS = 6
DIFF_HEADS = 4
DIFF_QK_DIM = HEAD_DIM // 2
NSA_HEADS = 6
NSA_KV_GROUPS = 2
NSA_HPG = NSA_HEADS // NSA_KV_GROUPS
CMP_LEN = 32
CMP_STRIDE = 16
SEL_LEN = 64
SEL_TOPN = 8
WINDOW = 512
N_BUCKETS = 32
MAX_DISTANCE = 128
D_FF = 5632
CONV_W = 3
EPS = 1e-6
NEG_INF = -1e30
FORCE_SCORE = 1e4

IN_SIZES = (
    FOX_HEADS * HEAD_DIM, FOX_HEADS * HEAD_DIM, FOX_HEADS * HEAD_DIM, FOX_HEADS,
    DIFF_HEADS * 2 * DIFF_QK_DIM, DIFF_HEADS * 2 * DIFF_QK_DIM, DIFF_HEADS * HEAD_DIM,
    NSA_HEADS * HEAD_DIM,
    NSA_KV_GROUPS * HEAD_DIM, NSA_KV_GROUPS * HEAD_DIM,
    NSA_KV_GROUPS * HEAD_DIM, NSA_KV_GROUPS * HEAD_DIM,
    NSA_KV_GROUPS * HEAD_DIM, NSA_KV_GROUPS * HEAD_DIM,
    NSA_HEADS * 3,
)
_IN_OFF = np.concatenate([[0], np.cumsum(IN_SIZES)])
(_O_FQ, _O_FK, _O_FV, _O_FF, _O_DQ, _O_DK, _O_DV, _O_NQ, _O_NKC, _O_NVC, _O_NKS, _O_NVS,
 _O_NKW, _O_NVW, _O_NG) = [int(v) for v in _IN_OFF[:-1]]

LANE = 128
HALO = 8
C_FQ, C_FK, C_FV, C_NQ = 0, 768, 1536, 2304
C_DQ, C_DK, C_DV = 3072, 3584, 4096
C_NKC, C_NVC, C_NKS, C_NVS, C_NKW, C_NVW = 4608, 4864, 5120, 5376, 5632, 5888
N_MAIN = 6144
N_GATE = 3 * LANE
ATT_T = 256
VMEM_LIMIT = 56 * 1024 * 1024

_F32 = jnp.float32
_BF16 = jnp.bfloat16


def _main_perm():
    segs = [(_O_FQ, 768), (_O_FK, 768), (_O_FV, 768), (_O_NQ, 768), (_O_DQ, 512), (_O_DK, 512),
            (_O_DV, 512), (_O_NKC, 256), (_O_NVC, 256), (_O_NKS, 256), (_O_NVS, 256),
            (_O_NKW, 256), (_O_NVW, 256)]
    return np.concatenate([np.arange(o, o + n) for o, n in segs]).astype(np.int32)


def _gate_perm():
    idx = np.zeros((N_GATE,), np.int32)
    valid = np.zeros((N_GATE,), np.float32)
    idx[:FOX_HEADS] = _O_FF + np.arange(FOX_HEADS)
    valid[:FOX_HEADS] = 1.0
    per_group = NSA_HPG * 3
    for g in range(NSA_KV_GROUPS):
        idx[(1 + g) * LANE:(1 + g) * LANE + per_group] = _O_NG + g * per_group + np.arange(per_group)
        valid[(1 + g) * LANE:(1 + g) * LANE + per_group] = 1.0
    return idx, valid


def _t5_bucket_np(rel):
    n = np.maximum(rel, 0)
    max_exact = N_BUCKETS // 2
    nf = np.maximum(n, 1).astype(np.float32)
    large = max_exact + (np.log(nf / np.float32(max_exact)) / np.float32(math.log(MAX_DISTANCE / max_exact))
                         * np.float32(N_BUCKETS - max_exact)).astype(np.int32)
    large = np.minimum(large, N_BUCKETS - 1)
    return np.where(n < max_exact, n, large).astype(np.int32)


def _toeplitz_bucket_tiles(t):
    i = np.arange(t)[:, None]
    j = np.arange(t)[None, :]
    d0 = np.where(j <= i, _t5_bucket_np(i - j), -1)
    d1 = _t5_bucket_np(i - j + t)
    far = _t5_bucket_np(np.full((t, t), 2 * t + 1))
    assert np.all(_t5_bucket_np(np.arange(t + 1, 8 * t)) == far[0, 0])
    assert WINDOW == 2 * t
    w2 = np.where(i < j, far, -1)
    return np.stack([d0, d1, far, w2]).astype(np.int32)


def _cmp_bucket_tile(s):
    n = np.arange(LANE)[None, :]
    t = np.arange(s)[:, None]
    n_cmp = (s - CMP_LEN) // CMP_STRIDE + 1
    block_end = n * CMP_STRIDE + CMP_LEN - 1
    ok = (block_end <= t) & (n < n_cmp)
    return np.where(ok, _t5_bucket_np(t - block_end), -1).astype(np.int32)


def _dot(a, b):
    return jnp.dot(a, b, preferred_element_type=_F32)


def _dot_nt(a, b):
    return lax.dot_general(a, b, (((1,), (1,)), ((), ())), preferred_element_type=_F32)


def _split3(x):
    hi = x.astype(_BF16)
    r = x - hi.astype(_F32)
    mid = r.astype(_BF16)
    lo = (r - mid.astype(_F32)).astype(_BF16)
    return hi, mid, lo


def _rms(x, g):
    return x * lax.rsqrt(jnp.mean(x * x, axis=-1, keepdims=True) + EPS) * g


def _cparams(sem, vmem=VMEM_LIMIT):
    return pltpu.CompilerParams(dimension_semantics=sem, vmem_limit_bytes=vmem)


def _inproj_kernel(x_ref, g_ref, w_ref, wg_ref, main_ref, gate_ref, xn_ref):
    @pl.when(pl.program_id(1) == 0)
    def _():
        xn = _rms(x_ref[...], g_ref[...]).astype(_BF16)
        xn_ref[...] = xn
        gate_ref[...] = _dot(xn, wg_ref[...])

    main_ref[...] = _dot(xn_ref[...], w_ref[...]).astype(_BF16)


def _inproj(h, g, w_main, w_gate, tm=512, tn=512):
    m = h.shape[0]
    return pl.pallas_call(
        _inproj_kernel,
        grid=(m // tm, N_MAIN // tn),
        in_specs=[
            pl.BlockSpec((tm, D_MODEL), lambda i, j: (i, 0)),
            pl.BlockSpec((1, D_MODEL), lambda i, j: (0, 0)),
            pl.BlockSpec((D_MODEL, tn), lambda i, j: (0, j)),
            pl.BlockSpec((D_MODEL, N_GATE), lambda i, j: (0, 0)),
        ],
        out_specs=[
            pl.BlockSpec((tm, tn), lambda i, j: (i, j)),
            pl.BlockSpec((tm, N_GATE), lambda i, j: (i, 0)),
        ],
        out_shape=[jax.ShapeDtypeStruct((m, N_MAIN), _BF16), jax.ShapeDtypeStruct((m, N_GATE), _F32)],
        scratch_shapes=[pltpu.VMEM((tm, D_MODEL), _BF16)],
        compiler_params=_cparams(("parallel", "arbitrary")),
        name="rms_inproj",
    )(h, g, w_main, w_gate)


def _bias_table_kernel(rel_ref, idx_ref, out_ref):
    h = pl.program_id(0)
    idx = idx_ref[...]
    acc = jnp.full(idx.shape, NEG_INF, _F32)
    for b in range(N_BUCKETS):
        acc = jnp.where(idx == b, rel_ref[b, h], acc)
    out_ref[...] = acc


def _bias_tables(rel_bias, idx, n_heads, head0):
    r, c = idx.shape
    return pl.pallas_call(
        _bias_table_kernel,
        grid=(n_heads,),
        in_specs=[
            pl.BlockSpec(memory_space=pltpu.SMEM),
            pl.BlockSpec((r, c), lambda h: (0, 0)),
        ],
        out_specs=pl.BlockSpec((None, r, c), lambda h: (h, 0, 0)),
        out_shape=jax.ShapeDtypeStruct((n_heads, r, c), _F32),
        compiler_params=_cparams(("arbitrary",)),
        name="t5_bias_tables",
    )(rel_bias[:, head0:head0 + n_heads], jnp.asarray(idx))


def _fox_prep_kernel(gate_ref, fb_ref, qa_ref, ka_ref, *, chunk):
    s = gate_ref.shape[0]
    row = lax.broadcasted_iota(jnp.int32, (chunk, chunk), 0)
    col = lax.broadcasted_iota(jnp.int32, (chunk, chunk), 1)
    tri = jnp.where(col <= row, 1.0, 0.0).astype(_BF16)
    lane = lax.broadcasted_iota(jnp.int32, (chunk, LANE), 1)
    carry = jnp.zeros((1, LANE), _F32)
    for c in range(s // chunk):
        x = gate_ref[c * chunk:(c + 1) * chunk, :] + fb_ref[...]
        log_f = jnp.minimum(x, 0.0) - jnp.log1p(jnp.exp(-jnp.abs(x)))
        hi, mid, lo = _split3(log_f)
        cum = (_dot(tri, hi) + _dot(tri, mid)) + _dot(tri, lo) + carry
        carry = cum[chunk - 1:chunk, :]
        for h in range(FOX_HEADS):
            chi, cmid, clo = [p.astype(_F32) for p in _split3(cum[:, h:h + 1])]
            qa = jnp.where(lane == 0, chi, jnp.where(lane == 1, cmid, jnp.where(lane == 2, clo,
                           jnp.where(lane < 6, 1.0, 0.0))))
            ka = jnp.where(lane < 3, 1.0, jnp.where(lane == 3, -chi, jnp.where(lane == 4, -cmid,
                           jnp.where(lane == 5, -clo, 0.0))))
            qa_ref[c * chunk:(c + 1) * chunk, h * LANE:(h + 1) * LANE] = qa.astype(_BF16)
            ka_ref[c * chunk:(c + 1) * chunk, h * LANE:(h + 1) * LANE] = ka.astype(_BF16)


def _fox_prep(gates3, f_bias_row):
    b, s, _ = gates3.shape
    w = FOX_HEADS * LANE
    return pl.pallas_call(
        functools.partial(_fox_prep_kernel, chunk=256),
        grid=(b,),
        in_specs=[
            pl.BlockSpec((None, s, LANE), lambda i: (i, 0, 0)),
            pl.BlockSpec((1, LANE), lambda i: (0, 0)),
        ],
        out_specs=[
            pl.BlockSpec((None, s, w), lambda i: (i, 0, 0)),
            pl.BlockSpec((None, s, w), lambda i: (i, 0, 0)),
        ],
        out_shape=[jax.ShapeDtypeStruct((b, s, w), _BF16), jax.ShapeDtypeStruct((b, s, w), _BF16)],
        compiler_params=_cparams(("parallel",)),
        name="fox_prep",
    )(gates3, f_bias_row)


def _online_update(s, v, m, l, acc):
    m_new = jnp.maximum(m, jnp.max(s, axis=-1, keepdims=True))
    alpha = jnp.exp(m - m_new)
    p = jnp.exp(s - m_new)
    l_new = alpha * l + jnp.sum(p, axis=-1, keepdims=True)
    acc_new = alpha * acc + _dot(p.astype(_BF16), v)
    return m_new, l_new, acc_new


def _softmax_init(rows):
    return (jnp.full((rows, 1), NEG_INF, _F32), jnp.zeros((rows, 1), _F32), jnp.zeros((rows, HEAD_DIM), _F32))


def _fox_attn_kernel(q_ref, k_ref, v_ref, qa_ref, ka_ref, o_ref, *, t):
    qi = pl.program_id(2)
    q = q_ref[...]
    qa = qa_ref[...]
    scale = HEAD_DIM ** -0.5

    def scores(kj):
        ks = pl.ds(pl.multiple_of(kj * t, t), t)
        return _dot_nt(q, k_ref[ks, :]) * scale + _dot_nt(qa, ka_ref[ks, :]), v_ref[ks, :]

    def body(kj, carry):
        s, v = scores(kj)
        return _online_update(s, v, *carry)

    carry = lax.fori_loop(0, qi, body, _softmax_init(t))
    s, v = scores(qi)
    row = lax.broadcasted_iota(jnp.int32, (t, t), 0)
    col = lax.broadcasted_iota(jnp.int32, (t, t), 1)
    s = jnp.where(col <= row, s, NEG_INF)
    _, l, acc = _online_update(s, v, *carry)
    o_ref[...] = (acc / l).astype(o_ref.dtype)


def _fox_attn(main3, qa, ka, t=ATT_T):
    b, s, _ = main3.shape
    cb = lambda c: c // LANE
    return pl.pallas_call(
        functools.partial(_fox_attn_kernel, t=t),
        grid=(b, FOX_HEADS, s // t),
        in_specs=[
            pl.BlockSpec((None, t, LANE), lambda i, h, q: (i, q, cb(C_FQ) + h)),
            pl.BlockSpec((None, s, LANE), lambda i, h, q: (i, 0, cb(C_FK) + h)),
            pl.BlockSpec((None, s, LANE), lambda i, h, q: (i, 0, cb(C_FV) + h)),
            pl.BlockSpec((None, t, LANE), lambda i, h, q: (i, q, h)),
            pl.BlockSpec((None, s, LANE), lambda i, h, q: (i, 0, h)),
        ],
        out_specs=pl.BlockSpec((None, t, LANE), lambda i, h, q: (i, q, h)),
        out_shape=jax.ShapeDtypeStruct((b, s, FOX_HEADS * HEAD_DIM), _BF16),
        compiler_params=_cparams(("parallel", "parallel", "arbitrary")),
        name="fox_attn",
    )(main3, main3, main3, qa, ka)


def _diff_attn_kernel(q_ref, k_ref, v_ref, bias_ref, lam_ref, g_ref, o_ref, *, t, lam_init):
    qi = pl.program_id(2)
    q = q_ref[...]
    lane = lax.broadcasted_iota(jnp.int32, (1, LANE), 1)
    first_map = jnp.where(lane < DIFF_QK_DIM, 1.0, 0.0).astype(q.dtype)
    q1 = q * first_map
    q2 = q - q1
    scale = DIFF_QK_DIM ** -0.5

    def body(kj, carry):
        c1, c2 = carry
        ks = pl.ds(pl.multiple_of(kj * t, t), t)
        k = k_ref[ks, :]
        v = v_ref[ks, :]
        bias = bias_ref[jnp.minimum(qi - kj, 2)]
        c1 = _online_update(_dot_nt(q1, k) * scale + bias, v, *c1)
        c2 = _online_update(_dot_nt(q2, k) * scale + bias, v, *c2)
        return c1, c2

    (_, l1, a1), (_, l2, a2) = lax.fori_loop(0, qi + 1, body, (_softmax_init(t), _softmax_init(t)))
    lv = lam_ref[...]
    lam = (jnp.exp(jnp.sum(lv[0:1, :] * lv[1:2, :], axis=-1, keepdims=True))
           - jnp.exp(jnp.sum(lv[2:3, :] * lv[3:4, :], axis=-1, keepdims=True)) + lam_init)
    o = a1 / l1 - lam * (a2 / l2)
    o_ref[...] = (_rms(o, g_ref[...]) * (1.0 - lam_init)).astype(o_ref.dtype)


def _diff_attn(main3, bias, lam_rows, subln_g, lam_init, t=ATT_T):
    b, s, _ = main3.shape
    cb = lambda c: c // LANE
    return pl.pallas_call(
        functools.partial(_diff_attn_kernel, t=t, lam_init=lam_init),
        grid=(b, DIFF_HEADS, s // t),
        in_specs=[
            pl.BlockSpec((None, t, LANE), lambda i, h, q: (i, q, cb(C_DQ) + h)),
            pl.BlockSpec((None, s, LANE), lambda i, h, q: (i, 0, cb(C_DK) + h)),
            pl.BlockSpec((None, s, LANE), lambda i, h, q: (i, 0, cb(C_DV) + h)),
            pl.BlockSpec((None, 4, t, t), lambda i, h, q: (h, 0, 0, 0)),
            pl.BlockSpec((8, LANE), lambda i, h, q: (0, 0)),
            pl.BlockSpec((1, LANE), lambda i, h, q: (0, 0)),
        ],
        out_specs=pl.BlockSpec((None, t, LANE), lambda i, h, q: (i, q, h)),
        out_shape=jax.ShapeDtypeStruct((b, s, DIFF_HEADS * HEAD_DIM), _BF16),
        compiler_params=_cparams(("parallel", "parallel", "arbitrary")),
        name="diff_attn",
    )(main3, main3, main3, bias, lam_rows, subln_g)


def _gelu_tanh(x):
    return 0.5 * x * (1.0 + jnp.tanh(math.sqrt(2.0 / math.pi) * (x + 0.044715 * (x * x * x))))


def _nsa_compress_kernel(rk_ref, rv_ref, pos_ref, wk1_ref, wk2_ref, wv1_ref, wv2_ref, kc_ref, vc_ref, *, n_cmp):
    half = CMP_STRIDE * HEAD_DIM
    pos = pos_ref[...]
    rows = rk_ref.shape[0]
    keep = lax.broadcasted_iota(jnp.int32, (rows, HEAD_DIM), 0) < n_cmp

    def compress(r_ref, w1_ref, w2_ref, out_ref):
        r = r_ref[...]
        first = _dot(r, w1_ref[:half, :])
        second = _dot(r, w1_ref[half:, :])
        pre = first + pltpu.roll(second, rows - 1, 0) + _dot(pos, w1_ref[...])
        out = _dot(_gelu_tanh(pre).astype(_BF16), w2_ref[...])
        out_ref[...] = jnp.where(keep, out, 0.0).astype(out_ref.dtype)

    compress(rk_ref, wk1_ref, wk2_ref, kc_ref)
    compress(rv_ref, wv1_ref, wv2_ref, vc_ref)


def _nsa_compress(rk, rv, pos_flat, wk1, wk2, wv1, wv2, n_cmp):
    b, g, rows, width = rk.shape
    blk = pl.BlockSpec((None, None, rows, width), lambda i, j: (i, j, 0, 0))
    full = lambda a: pl.BlockSpec(a.shape, lambda i, j: (0,) * a.ndim)
    out_blk = pl.BlockSpec((None, None, rows, HEAD_DIM), lambda i, j: (i, j, 0, 0))
    out_sd = jax.ShapeDtypeStruct((b, g, rows, HEAD_DIM), _BF16)
    return pl.pallas_call(
        functools.partial(_nsa_compress_kernel, n_cmp=n_cmp),
        grid=(b, g),
        in_specs=[blk, blk, full(pos_flat), full(wk1), full(wk2), full(wv1), full(wv2)],
        out_specs=[out_blk, out_blk],
        out_shape=[out_sd, out_sd],
        compiler_params=_cparams(("parallel", "parallel")),
        name="nsa_compress",
    )(rk, rv, pos_flat, wk1, wk2, wv1, wv2)


def _nsa_attn_kernel(q_ref, kc_ref, vc_ref, ks_ref, vs_ref, kw_ref, vw_ref, gate_ref, toep_ref, cmpb_ref,
                     ovt_ref, exp_ref, o_ref, *, t):
    qi = pl.program_id(2)
    t0 = qi * t
    scale = HEAD_DIM ** -0.5
    n_sel = ovt_ref.shape[0]
    qh = [q_ref[:, h * HEAD_DIM:(h + 1) * HEAD_DIM] for h in range(NSA_HPG)]

    tpos = t0 + lax.broadcasted_iota(jnp.int32, (t, 1), 0)
    row_ok = jnp.where(tpos >= CMP_LEN - 1, 1.0, 0.0)
    kc = kc_ref[...]
    vc = vc_ref[...]
    o_cmp = []
    psum = jnp.zeros((t, LANE), _F32)
    for h in range(NSA_HPG):
        sc = _dot_nt(qh[h], kc) * scale + cmpb_ref[h]
        e = jnp.exp(sc - jnp.max(sc, axis=-1, keepdims=True))
        p = e / jnp.sum(e, axis=-1, keepdims=True) * row_ok
        o_cmp.append(_dot(p.astype(_BF16), vc))
        psum = psum + p

    ovt = ovt_ref[...]
    p_hi, p_mid, p_lo = _split3(psum)
    imp = (_dot_nt(ovt, p_hi) + _dot_nt(ovt, p_mid)) + _dot_nt(ovt, p_lo)
    j = lax.broadcasted_iota(jnp.int32, (n_sel, t), 0)
    blk_t = (t0 + lax.broadcasted_iota(jnp.int32, (n_sel, t), 1)) // SEL_LEN
    valid = j <= blk_t
    forced = (j == 0) | (j == blk_t) | (j == blk_t - 1)
    score = jnp.where(valid, imp + jnp.where(forced, FORCE_SCORE, 0.0), NEG_INF)
    rank = jnp.zeros((n_sel, t), _F32)
    for k in range(n_sel):
        sk = score[k:k + 1, :]
        ahead = (sk > score) | ((sk == score) & (j > k))
        rank = rank + jnp.where(ahead, 1.0, 0.0)
    sel_t = jnp.where(valid & (rank < SEL_TOPN), 1.0, 0.0)
    sel_t = jnp.concatenate([sel_t, jnp.zeros((LANE - n_sel, t), _F32)], axis=0)
    sel = jnp.transpose(sel_t).astype(_BF16)

    q_all = jnp.concatenate(qh, axis=0)
    rows = NSA_HPG * t

    def bias_rows(which):
        return jnp.concatenate([toep_ref[h, which] for h in range(NSA_HPG)], axis=0)

    def sel_body(kj, carry):
        ks = pl.ds(pl.multiple_of(kj * t, t), t)
        s = _dot_nt(q_all, ks_ref[ks, :]) * scale + bias_rows(jnp.minimum(qi - kj, 2))
        chosen = _dot(sel, exp_ref[:, ks])
        chosen = jnp.concatenate([chosen] * NSA_HPG, axis=0)
        s = jnp.where(chosen > 0.5, s, NEG_INF)
        return _online_update(s, vs_ref[ks, :], *carry)

    _, l_s, a_s = lax.fori_loop(0, qi + 1, sel_body, _softmax_init(rows))
    o_sel = a_s / l_s

    def win_body(kj, carry):
        ks = pl.ds(pl.multiple_of(kj * t, t), t)
        d = qi - kj
        s = _dot_nt(q_all, kw_ref[ks, :]) * scale + bias_rows(jnp.where(d == 2, 3, d))
        return _online_update(s, vw_ref[ks, :], *carry)

    _, l_w, a_w = lax.fori_loop(jnp.maximum(qi - 2, 0), qi + 1, win_body, _softmax_init(rows))
    o_win = a_w / l_w

    gl = gate_ref[...]
    gates = 1.0 / (1.0 + jnp.exp(-gl))
    for h in range(NSA_HPG):
        out = (gates[:, 3 * h:3 * h + 1] * o_cmp[h]
               + gates[:, 3 * h + 1:3 * h + 2] * o_sel[h * t:(h + 1) * t, :]
               + gates[:, 3 * h + 2:3 * h + 3] * o_win[h * t:(h + 1) * t, :])
        o_ref[:, h * HEAD_DIM:(h + 1) * HEAD_DIM] = out.astype(o_ref.dtype)


def _nsa_attn(main3, kcmp, vcmp, gates3, toep, cmpb, ovt, expand, t=ATT_T):
    b, s, _ = main3.shape
    cb = lambda c: c // LANE
    gw = NSA_HPG * HEAD_DIM
    kv = lambda c: pl.BlockSpec((None, s, LANE), lambda i, g, q: (i, 0, cb(c) + g))
    cmp_blk = pl.BlockSpec((None, None, LANE, HEAD_DIM), lambda i, g, q: (i, g, 0, 0))
    return pl.pallas_call(
        functools.partial(_nsa_attn_kernel, t=t),
        grid=(b, NSA_KV_GROUPS, s // t),
        in_specs=[
            pl.BlockSpec((None, t, gw), lambda i, g, q: (i, q, C_NQ // gw + g)),
            cmp_blk, cmp_blk,
            kv(C_NKS), kv(C_NVS), kv(C_NKW), kv(C_NVW),
            pl.BlockSpec((None, t, LANE), lambda i, g, q: (i, q, 1 + g)),
            pl.BlockSpec((NSA_HPG, 4, t, t), lambda i, g, q: (g, 0, 0, 0)),
            pl.BlockSpec((NSA_HPG, t, LANE), lambda i, g, q: (g, q, 0)),
            pl.BlockSpec(ovt.shape, lambda i, g, q: (0, 0)),
            pl.BlockSpec(expand.shape, lambda i, g, q: (0, 0)),
        ],
        out_specs=pl.BlockSpec((None, t, gw), lambda i, g, q: (i, q, g)),
        out_shape=jax.ShapeDtypeStruct((b, s, NSA_HEADS * HEAD_DIM), _BF16),
        compiler_params=_cparams(("parallel", "parallel", "arbitrary")),
        name="nsa_attn",
    )(main3, kcmp, vcmp, main3, main3, main3, main3, gates3, toep, cmpb, ovt, expand)


def _outproj_kernel(a1_ref, a2_ref, a3_ref, w1_ref, w2_ref, w3_ref, h_ref, o_ref):
    o_ref[...] = h_ref[...] + ((_dot(a1_ref[...], w1_ref[...]) + _dot(a2_ref[...], w2_ref[...]))
                               + _dot(a3_ref[...], w3_ref[...]))


def _outproj(a1, a2, a3, w1, w2, w3, h, tm=512):
    m = h.shape[0]
    act = lambda a: pl.BlockSpec((tm, a.shape[1]), lambda i: (i, 0))
    full = lambda a: pl.BlockSpec(a.shape, lambda i: (0, 0))
    return pl.pallas_call(
        _outproj_kernel,
        grid=(m // tm,),
        in_specs=[act(a1), act(a2), act(a3), full(w1), full(w2), full(w3), act(h)],
        out_specs=act(h),
        out_shape=jax.ShapeDtypeStruct(h.shape, _F32),
        compiler_params=_cparams(("parallel",)),
        name="outproj_residual",
    )(a1, a2, a3, w1, w2, w3, h)


def _ffn_kernel(x_ref, halo_ref, g_ref, wg_ref, wu_ref, cwg_ref, cwu_ref, cbg_ref, cbu_ref, wd_ref, o_ref,
                xn_ref, acc_ref, hg_ref, hu_ref, *, tm, seq):
    i = pl.program_id(0)
    j = pl.program_id(1)

    @pl.when(j == 0)
    def _():
        g = g_ref[...]
        xn_ref[HALO:, :] = _rms(x_ref[...], g).astype(_BF16)
        keep = jnp.where((i * tm) % seq == 0, 0.0, 1.0)
        xn_ref[:HALO, :] = (_rms(halo_ref[0], g) * keep).astype(_BF16)
        acc_ref[...] = jnp.zeros_like(acc_ref)

    xn = xn_ref[...]
    hg_ref[...] = _dot(xn, wg_ref[...])
    hu_ref[...] = _dot(xn, wu_ref[...])

    def conv(h_ref, cw_ref, cb_ref):
        out = cb_ref[...] + cw_ref[CONV_W - 1:CONV_W, :] * h_ref[HALO:, :]
        for tap in range(CONV_W - 1):
            shift = CONV_W - 1 - tap
            out = out + cw_ref[tap:tap + 1, :] * h_ref[HALO - shift:HALO - shift + tm, :]
        return out

    gate = conv(hg_ref, cwg_ref, cbg_ref)
    up = conv(hu_ref, cwu_ref, cbu_ref)
    act = gate / (1.0 + jnp.exp(-gate)) * up
    acc_ref[...] += _dot(act.astype(_BF16), wd_ref[...])

    @pl.when(j == pl.num_programs(1) - 1)
    def _():
        o_ref[...] = x_ref[...] + acc_ref[...]


def _ffn(h, g, w_up, conv_w, conv_b, w_down, seq, tm=512, tf=512):
    m = h.shape[0]
    nf = D_FF // tf
    halo_view = h.reshape(m // HALO, HALO, D_MODEL)
    return pl.pallas_call(
        functools.partial(_ffn_kernel, tm=tm, seq=seq),
        grid=(m // tm, nf),
        in_specs=[
            pl.BlockSpec((tm, D_MODEL), lambda i, j: (i, 0)),
            pl.BlockSpec((1, HALO, D_MODEL), lambda i, j: (jnp.maximum(i * (tm // HALO) - 1, 0), 0, 0)),
            pl.BlockSpec((1, D_MODEL), lambda i, j: (0, 0)),
            pl.BlockSpec((D_MODEL, tf), lambda i, j: (0, j)),
            pl.BlockSpec((D_MODEL, tf), lambda i, j: (0, j + nf)),
            pl.BlockSpec((CONV_W, tf), lambda i, j: (0, j)),
            pl.BlockSpec((CONV_W, tf), lambda i, j: (0, j + nf)),
            pl.BlockSpec((1, tf), lambda i, j: (0, j)),
            pl.BlockSpec((1, tf), lambda i, j: (0, j + nf)),
            pl.BlockSpec((tf, D_MODEL), lambda i, j: (j, 0)),
        ],
        out_specs=pl.BlockSpec((tm, D_MODEL), lambda i, j: (i, 0)),
        out_shape=jax.ShapeDtypeStruct(h.shape, _F32),
        scratch_shapes=[
            pltpu.VMEM((tm + HALO, D_MODEL), _BF16),
            pltpu.VMEM((tm, D_MODEL), _F32),
            pltpu.VMEM((tm + HALO, tf), _F32),
            pltpu.VMEM((tm + HALO, tf), _F32),
        ],
        compiler_params=_cparams(("parallel", "arbitrary")),
        name="rms_conv_ffn",
    )(h, halo_view, g, w_up, w_up, conv_w, conv_w, conv_b, conv_b, w_down)


def _final_norm_kernel(x_ref, g_ref, o_ref):
    o_ref[...] = _rms(x_ref[...], g_ref[...])


def _final_norm(h, g, tm=512):
    m = h.shape[0]
    return pl.pallas_call(
        _final_norm_kernel,
        grid=(m // tm,),
        in_specs=[pl.BlockSpec((tm, D_MODEL), lambda i: (i, 0)), pl.BlockSpec((1, D_MODEL), lambda i: (0, 0))],
        out_specs=pl.BlockSpec((tm, D_MODEL), lambda i: (i, 0)),
        out_shape=jax.ShapeDtypeStruct(h.shape, _F32),
        compiler_params=_cparams(("parallel",)),
        name="final_rmsnorm",
    )(h, g)


def _pad_lanes(v, width=LANE):
    return jnp.pad(v, ((0, 0), (0, width - v.shape[-1])))


def kernel(x, attn_norm_g, w_in, fox_f_bias, diff_lq1, diff_lk1, diff_lq2, diff_lk2, diff_subln_g, nsa_cmp_pos, nsa_cmp_wk1, nsa_cmp_wk2, nsa_cmp_wv1, nsa_cmp_wv2, w_out, ffn_norm_g, ffn_w_up, ffn_conv_w, ffn_conv_b, ffn_w_down, rel_bias, final_norm_g):
    b, s, _ = x.shape
    depth = w_in.shape[0]
    m = b * s
    t = ATT_T
    n_cmp = (s - CMP_LEN) // CMP_STRIDE + 1
    n_sel = s // SEL_LEN
    assert n_cmp + 1 == s // CMP_STRIDE == LANE and n_sel >= SEL_TOPN and s % t == 0

    gate_idx, gate_valid = _gate_perm()
    w_main = jnp.take(w_in, jnp.asarray(_main_perm()), axis=2).astype(_BF16)
    w_gate = (jnp.take(w_in, jnp.asarray(gate_idx), axis=2) * jnp.asarray(gate_valid)).astype(_BF16)
    w_out_b = w_out.astype(_BF16)
    w_up_b = ffn_w_up.astype(_BF16)
    w_down_b = ffn_w_down.astype(_BF16)
    wk1, wk2 = nsa_cmp_wk1.astype(_BF16), nsa_cmp_wk2.astype(_BF16)
    wv1, wv2 = nsa_cmp_wv1.astype(_BF16), nsa_cmp_wv2.astype(_BF16)
    pos_flat = nsa_cmp_pos.reshape(depth, 1, CMP_LEN * HEAD_DIM).astype(_BF16)
    f_bias_rows = _pad_lanes(fox_f_bias)
    lam_rows = jnp.stack([_pad_lanes(v) for v in (diff_lq1, diff_lk1, diff_lq2, diff_lk2)], axis=1)
    lam_rows = jnp.pad(lam_rows, ((0, 0), (0, 4), (0, 0)))

    toep_idx = _toeplitz_bucket_tiles(t).reshape(4 * t, t)
    diff_bias = _bias_tables(rel_bias, toep_idx, DIFF_HEADS, 0).reshape(DIFF_HEADS, 4, t, t)
    nsa_bias = _bias_tables(rel_bias, toep_idx, NSA_HEADS, DIFF_HEADS).reshape(NSA_HEADS, 4, t, t)
    cmp_bias = _bias_tables(rel_bias, _cmp_bucket_tile(s), NSA_HEADS, DIFF_HEADS)
    cmp_starts = np.arange(LANE) * CMP_STRIDE
    sel_starts = np.arange(n_sel) * SEL_LEN
    overlap = np.clip(np.minimum(cmp_starts[:, None] + CMP_LEN, sel_starts[None, :] + SEL_LEN)
                      - np.maximum(cmp_starts[:, None], sel_starts[None, :]), 0, None).astype(np.float32) / CMP_LEN
    overlap[n_cmp:] = 0.0
    ovt = jnp.asarray(overlap.T, _BF16)
    expand = np.zeros((LANE, s), np.float32)
    expand[np.arange(s) // SEL_LEN, np.arange(s)] = 1.0
    expand = jnp.asarray(expand, _BF16)

    def cmp_rows(main3, c):
        a = main3[:, :, c:c + NSA_KV_GROUPS * HEAD_DIM]
        a = a.reshape(b, s // CMP_STRIDE, CMP_STRIDE, NSA_KV_GROUPS, HEAD_DIM)
        return jnp.transpose(a, (0, 3, 1, 2, 4)).reshape(b, NSA_KV_GROUPS, s // CMP_STRIDE, CMP_STRIDE * HEAD_DIM)

    h = x.reshape(m, D_MODEL)
    for l in range(depth):
        main, gates = _inproj(h, attn_norm_g[l][None], w_main[l], w_gate[l])
        main3 = main.reshape(b, s, N_MAIN)
        gates3 = gates.reshape(b, s, N_GATE)

        qa, ka = _fox_prep(gates3, f_bias_rows[l][None])
        fox_o = _fox_attn(main3, qa, ka)

        lam_init = 0.8 - 0.6 * math.exp(-0.3 * l)
        diff_o = _diff_attn(main3, diff_bias, lam_rows[l], diff_subln_g[l][None], lam_init)

        kcmp, vcmp = _nsa_compress(cmp_rows(main3, C_NKC), cmp_rows(main3, C_NVC), pos_flat[l],
                                   wk1[l], wk2[l], wv1[l], wv2[l], n_cmp)
        nsa_o = _nsa_attn(main3, kcmp, vcmp, gates3, nsa_bias, cmp_bias, ovt, expand)

        wo = w_out_b[l]
        h = _outproj(fox_o.reshape(m, -1), diff_o.reshape(m, -1), nsa_o.reshape(m, -1),
                     wo[:768], wo[768:1280], wo[1280:], h)
        h = _ffn(h, ffn_norm_g[l][None], w_up_b[l], ffn_conv_w[l], ffn_conv_b[l][None], w_down_b[l], s)
    return _final_norm(h, final_norm_g[None]).reshape(b, s, D_MODEL)
```

```python
import functools
import math

import jax
import jax.numpy as jnp
import numpy as np
from jax import lax
from jax.experimental import pallas as pl
from jax.experimental.pallas import tpu as pltpu

D_MODEL = 2048
HEAD_DIM = 128
FOX_HEADS = 6
DIFF_HEADS = 4
DIFF_QK_DIM = HEAD_DIM // 2
NSA_HEADS = 6
NSA_KV_GROUPS = 2
NSA_HPG = NSA_HEADS // NSA_KV_GROUPS
CMP_LEN = 32
CMP_STRIDE = 16
SEL_LEN = 64
SEL_TOPN = 8
WINDOW = 512
N_BUCKETS = 32
MAX_DISTANCE = 128
D_FF = 5632
CONV_W = 3
EPS = 1e-6
NEG_INF = -1e30
FORCE_SCORE = 1e4

IN_SIZES = (
    FOX_HEADS * HEAD_DIM, FOX_HEADS * HEAD_DIM, FOX_HEADS * HEAD_DIM, FOX_HEADS,
    DIFF_HEADS * 2 * DIFF_QK_DIM, DIFF_HEADS * 2 * DIFF_QK_DIM, DIFF_HEADS * HEAD_DIM,
    NSA_HEADS * HEAD_DIM,
    NSA_KV_GROUPS * HEAD_DIM, NSA_KV_GROUPS * HEAD_DIM,
    NSA_KV_GROUPS * HEAD_DIM, NSA_KV_GROUPS * HEAD_DIM,
    NSA_KV_GROUPS * HEAD_DIM, NSA_KV_GROUPS * HEAD_DIM,
    NSA_HEADS * 3,
)
_IN_OFF = np.concatenate([[0], np.cumsum(IN_SIZES)])
(_O_FQ, _O_FK, _O_FV, _O_FF, _O_DQ, _O_DK, _O_DV, _O_NQ, _O_NKC, _O_NVC, _O_NKS, _O_NVS,
 _O_NKW, _O_NVW, _O_NG) = [int(v) for v in _IN_OFF[:-1]]

LANE = 128
HALO = 8
C_FQ, C_FK, C_FV, C_NQ = 0, 768, 1536, 2304
C_DQ, C_DK, C_DV = 3072, 3584, 4096
C_NKC, C_NVC, C_NKS, C_NVS, C_NKW, C_NVW = 4608, 4864, 5120, 5376, 5632, 5888
N_MAIN = 6144
N_GATE = 3 * LANE
ATT_T = 512
LOG2E = math.log2(math.e)
VMEM_LIMIT = 56 * 1024 * 1024

_F32 = jnp.float32
_BF16 = jnp.bfloat16


def _main_perm():
    segs = [(_O_FQ, 768), (_O_FK, 768), (_O_FV, 768), (_O_NQ, 768), (_O_DQ, 512), (_O_DK, 512),
            (_O_DV, 512), (_O_NKC, 256), (_O_NVC, 256), (_O_NKS, 256), (_O_NVS, 256),
            (_O_NKW, 256), (_O_NVW, 256)]
    return np.concatenate([np.arange(o, o + n) for o, n in segs]).astype(np.int32)


def _gate_perm():
    idx = np.zeros((N_GATE,), np.int32)
    valid = np.zeros((N_GATE,), np.float32)
    idx[:FOX_HEADS] = _O_FF + np.arange(FOX_HEADS)
    valid[:FOX_HEADS] = 1.0
    per_group = NSA_HPG * 3
    for g in range(NSA_KV_GROUPS):
        idx[(1 + g) * LANE:(1 + g) * LANE + per_group] = _O_NG + g * per_group + np.arange(per_group)
        valid[(1 + g) * LANE:(1 + g) * LANE + per_group] = 1.0
    return idx, valid


def _t5_bucket_np(rel):
    n = np.maximum(rel, 0)
    max_exact = N_BUCKETS // 2
    nf = np.maximum(n, 1).astype(np.float32)
    large = max_exact + (np.log(nf / np.float32(max_exact)) / np.float32(math.log(MAX_DISTANCE / max_exact))
                         * np.float32(N_BUCKETS - max_exact)).astype(np.int32)
    large = np.minimum(large, N_BUCKETS - 1)
    return np.where(n < max_exact, n, large).astype(np.int32)


def _toeplitz_bucket_tiles(t):
    i = np.arange(t)[:, None]
    j = np.arange(t)[None, :]
    d0 = np.where(j <= i, _t5_bucket_np(i - j), -1)
    d1 = _t5_bucket_np(i - j + t)
    assert np.all(_t5_bucket_np(np.arange(t + 1, 8 * t)) == N_BUCKETS - 1)
    assert WINDOW in (t, 2 * t)
    edge = np.where(i < j, _t5_bucket_np(i - j + WINDOW), -1)
    return np.stack([d0, d1, edge]).astype(np.int32)


def _cmp_bucket_tile(s):
    n = np.arange(LANE)[None, :]
    t = np.arange(s)[:, None]
    n_cmp = (s - CMP_LEN) // CMP_STRIDE + 1
    block_end = n * CMP_STRIDE + CMP_LEN - 1
    ok = (block_end <= t) & (n < n_cmp)
    return np.where(ok, _t5_bucket_np(t - block_end), -1).astype(np.int32)


def _dot(a, b):
    return jnp.dot(a, b, preferred_element_type=_F32)


def _dot_nt(a, b):
    return lax.dot_general(a, b, (((1,), (1,)), ((), ())), preferred_element_type=_F32)


def _split3(x):
    hi = x.astype(_BF16)
    r = x - hi.astype(_F32)
    mid = r.astype(_BF16)
    lo = (r - mid.astype(_F32)).astype(_BF16)
    return hi, mid, lo


def _rms(x, g):
    return x * lax.rsqrt(jnp.mean(x * x, axis=-1, keepdims=True) + EPS) * g


def _cparams(sem, vmem=VMEM_LIMIT):
    return pltpu.CompilerParams(dimension_semantics=sem, vmem_limit_bytes=vmem)


def _inproj_kernel(x_ref, g_ref, w_ref, wg_ref, cs_ref, main_ref, gate_ref, xn_ref):
    @pl.when(pl.program_id(1) == 0)
    def _():
        xn = _rms(x_ref[...], g_ref[...]).astype(_BF16)
        xn_ref[...] = xn
        gate_ref[...] = _dot(xn, wg_ref[...])

    main_ref[...] = (_dot(xn_ref[...], w_ref[...]) * cs_ref[...]).astype(_BF16)


def _query_col_scale():
    cs = np.ones((1, N_MAIN), np.float32)
    cs[:, C_FQ:C_FQ + FOX_HEADS * HEAD_DIM] = HEAD_DIM ** -0.5 * LOG2E
    cs[:, C_NQ:C_NQ + NSA_HEADS * HEAD_DIM] = HEAD_DIM ** -0.5 * LOG2E
    cs[:, C_DQ:C_DQ + DIFF_HEADS * 2 * DIFF_QK_DIM] = DIFF_QK_DIM ** -0.5 * LOG2E
    return cs


def _inproj(h, g, w_main, w_gate, tm=512, tn=512):
    m = h.shape[0]
    return pl.pallas_call(
        _inproj_kernel,
        grid=(m // tm, N_MAIN // tn),
        in_specs=[
            pl.BlockSpec((tm, D_MODEL), lambda i, j: (i, 0)),
            pl.BlockSpec((1, D_MODEL), lambda i, j: (0, 0)),
            pl.BlockSpec((D_MODEL, tn), lambda i, j: (0, j)),
            pl.BlockSpec((D_MODEL, N_GATE), lambda i, j: (0, 0)),
            pl.BlockSpec((1, tn), lambda i, j: (0, j)),
        ],
        out_specs=[
            pl.BlockSpec((tm, tn), lambda i, j: (i, j)),
            pl.BlockSpec((tm, N_GATE), lambda i, j: (i, 0)),
        ],
        out_shape=[jax.ShapeDtypeStruct((m, N_MAIN), _BF16), jax.ShapeDtypeStruct((m, N_GATE), _F32)],
        scratch_shapes=[pltpu.VMEM((tm, D_MODEL), _BF16)],
        compiler_params=_cparams(("parallel", "arbitrary")),
        name="rms_inproj",
    )(h, g, w_main, w_gate, jnp.asarray(_query_col_scale()))


def _bias_table_kernel(rel_ref, idx_ref, out_ref):
    h = pl.program_id(0)
    idx = idx_ref[...]
    acc = jnp.full(idx.shape, NEG_INF, _F32)
    last = rel_ref[N_BUCKETS - 1, h]
    for b in range(N_BUCKETS):
        acc = jnp.where(idx == b, (rel_ref[b, h] - last) * LOG2E, acc)
    out_ref[...] = acc


def _bias_tables(rel_bias, idx, n_heads, head0):
    r, c = idx.shape
    return pl.pallas_call(
        _bias_table_kernel,
        grid=(n_heads,),
        in_specs=[
            pl.BlockSpec(memory_space=pltpu.SMEM),
            pl.BlockSpec((r, c), lambda h: (0, 0)),
        ],
        out_specs=pl.BlockSpec((None, r, c), lambda h: (h, 0, 0)),
        out_shape=jax.ShapeDtypeStruct((n_heads, r, c), _F32),
        compiler_params=_cparams(("arbitrary",)),
        name="t5_bias_tables",
    )(rel_bias[:, head0:head0 + n_heads], jnp.asarray(idx))


def _fox_prep_kernel(gate_ref, fb_ref, qa_ref, ka_ref, *, chunk):
    s = gate_ref.shape[0]
    row = lax.broadcasted_iota(jnp.int32, (chunk, chunk), 0)
    col = lax.broadcasted_iota(jnp.int32, (chunk, chunk), 1)
    tri = jnp.where(col <= row, 1.0, 0.0).astype(_BF16)
    lane = lax.broadcasted_iota(jnp.int32, (chunk, LANE), 1)
    carry = jnp.zeros((1, LANE), _F32)
    for c in range(s // chunk):
        x = gate_ref[c * chunk:(c + 1) * chunk, :] + fb_ref[...]
        log_f = jnp.minimum(x, 0.0) - jnp.log1p(jnp.exp(-jnp.abs(x)))
        hi, mid, lo = _split3(log_f)
        cum = (_dot(tri, hi) + _dot(tri, mid)) + _dot(tri, lo) + carry
        carry = cum[chunk - 1:chunk, :]
        for h in range(FOX_HEADS):
            chi, cmid, clo = [p.astype(_F32) for p in _split3(cum[:, h:h + 1] * LOG2E)]
            qa = jnp.where(lane == 0, chi, jnp.where(lane == 1, cmid, jnp.where(lane == 2, clo,
                           jnp.where(lane < 6, 1.0, 0.0))))
            ka = jnp.where(lane < 3, 1.0, jnp.where(lane == 3, -chi, jnp.where(lane == 4, -cmid,
                           jnp.where(lane == 5, -clo, 0.0))))
            qa_ref[c * chunk:(c + 1) * chunk, h * LANE:(h + 1) * LANE] = qa.astype(_BF16)
            ka_ref[c * chunk:(c + 1) * chunk, h * LANE:(h + 1) * LANE] = ka.astype(_BF16)


def _fox_prep(gates3, f_bias_row):
    b, s, _ = gates3.shape
    w = FOX_HEADS * LANE
    return pl.pallas_call(
        functools.partial(_fox_prep_kernel, chunk=256),
        grid=(b,),
        in_specs=[
            pl.BlockSpec((None, s, LANE), lambda i: (i, 0, 0)),
            pl.BlockSpec((1, LANE), lambda i: (0, 0)),
        ],
        out_specs=[
            pl.BlockSpec((None, s, w), lambda i: (i, 0, 0)),
            pl.BlockSpec((None, s, w), lambda i: (i, 0, 0)),
        ],
        out_shape=[jax.ShapeDtypeStruct((b, s, w), _BF16), jax.ShapeDtypeStruct((b, s, w), _BF16)],
        compiler_params=_cparams(("parallel",)),
        name="fox_prep",
    )(gates3, f_bias_row)


def _online_update(s, v, m, l, acc):
    m_new = jnp.maximum(m, jnp.max(s, axis=-1, keepdims=True))
    alpha = jnp.exp2(m - m_new)
    p = jnp.exp2(s - m_new)
    l_new = alpha * l + jnp.sum(p, axis=-1, keepdims=True)
    acc_new = alpha * acc + _dot(p.astype(_BF16), v)
    return m_new, l_new, acc_new


def _softmax_init(rows):
    return (jnp.full((rows, 1), NEG_INF, _F32), jnp.zeros((rows, 1), _F32), jnp.zeros((rows, HEAD_DIM), _F32))


def _fox_attn_kernel(q_ref, k_ref, v_ref, qa_ref, ka_ref, o_ref, *, t):
    qi = pl.program_id(2)
    q = jnp.concatenate([q_ref[...], qa_ref[...]], axis=1)

    def scores(kj):
        ks = pl.ds(pl.multiple_of(kj * t, t), t)
        return _dot_nt(q, jnp.concatenate([k_ref[ks, :], ka_ref[ks, :]], axis=1)), v_ref[ks, :]

    def body(kj, carry):
        s, v = scores(kj)
        return _online_update(s, v, *carry)

    carry = lax.fori_loop(0, qi, body, _softmax_init(t))
    s, v = scores(qi)
    row = lax.broadcasted_iota(jnp.int32, (t, t), 0)
    col = lax.broadcasted_iota(jnp.int32, (t, t), 1)
    s = jnp.where(col <= row, s, NEG_INF)
    _, l, acc = _online_update(s, v, *carry)
    o_ref[...] = (acc / l).astype(o_ref.dtype)


def _fox_attn(main3, qa, ka, t=ATT_T):
    b, s, _ = main3.shape
    cb = lambda c: c // LANE
    return pl.pallas_call(
        functools.partial(_fox_attn_kernel, t=t),
        grid=(b, FOX_HEADS, s // t),
        in_specs=[
            pl.BlockSpec((None, t, LANE), lambda i, h, q: (i, q, cb(C_FQ) + h)),
            pl.BlockSpec((None, s, LANE), lambda i, h, q: (i, 0, cb(C_FK) + h)),
            pl.BlockSpec((None, s, LANE), lambda i, h, q: (i, 0, cb(C_FV) + h)),
            pl.BlockSpec((None, t, LANE), lambda i, h, q: (i, q, h)),
            pl.BlockSpec((None, s, LANE), lambda i, h, q: (i, 0, h)),
        ],
        out_specs=pl.BlockSpec((None, t, LANE), lambda i, h, q: (i, q, h)),
        out_shape=jax.ShapeDtypeStruct((b, s, FOX_HEADS * HEAD_DIM), _BF16),
        compiler_params=_cparams(("parallel", "parallel", "arbitrary")),
        name="fox_attn",
    )(main3, main3, main3, qa, ka)


def _diff_attn_kernel(q_ref, k_ref, v_ref, bias_ref, lam_ref, g_ref, o_ref, *, t, lam_init):
    qi = pl.program_id(2)
    q = q_ref[...]
    lane = lax.broadcasted_iota(jnp.int32, (1, LANE), 1)
    first_map = jnp.where(lane < DIFF_QK_DIM, 1.0, 0.0).astype(q.dtype)
    q1 = q * first_map
    q_both = jnp.concatenate([q1, q - q1], axis=0)

    def step(kj, carry, near):
        ks = pl.ds(pl.multiple_of(kj * t, t), t)
        s = _dot_nt(q_both, k_ref[ks, :])
        if near:
            s = (s.reshape(2, t, t) + bias_ref[qi - kj][None]).reshape(2 * t, t)
        return _online_update(s, v_ref[ks, :], *carry)

    n_far = jnp.maximum(qi - 1, 0)
    carry = lax.fori_loop(0, n_far, functools.partial(step, near=False), _softmax_init(2 * t))
    _, l, a = lax.fori_loop(n_far, qi + 1, functools.partial(step, near=True), carry)
    lv = lam_ref[...]
    lam = (jnp.exp(jnp.sum(lv[0:1, :] * lv[1:2, :], axis=-1, keepdims=True))
           - jnp.exp(jnp.sum(lv[2:3, :] * lv[3:4, :], axis=-1, keepdims=True)) + lam_init)
    o = a[:t] / l[:t] - lam * (a[t:] / l[t:])
    o_ref[...] = (_rms(o, g_ref[...]) * (1.0 - lam_init)).astype(o_ref.dtype)


def _diff_attn(main3, bias, lam_rows, subln_g, lam_init, t=ATT_T):
    b, s, _ = main3.shape
    cb = lambda c: c // LANE
    return pl.pallas_call(
        functools.partial(_diff_attn_kernel, t=t, lam_init=lam_init),
        grid=(DIFF_HEADS, b, s // t),
        in_specs=[
            pl.BlockSpec((None, t, LANE), lambda h, i, q: (i, q, cb(C_DQ) + h)),
            pl.BlockSpec((None, s, LANE), lambda h, i, q: (i, 0, cb(C_DK) + h)),
            pl.BlockSpec((None, s, LANE), lambda h, i, q: (i, 0, cb(C_DV) + h)),
            pl.BlockSpec((None, 3, t, t), lambda h, i, q: (h, 0, 0, 0)),
            pl.BlockSpec((8, LANE), lambda h, i, q: (0, 0)),
            pl.BlockSpec((1, LANE), lambda h, i, q: (0, 0)),
        ],
        out_specs=pl.BlockSpec((None, t, LANE), lambda h, i, q: (i, q, h)),
        out_shape=jax.ShapeDtypeStruct((b, s, DIFF_HEADS * HEAD_DIM), _BF16),
        compiler_params=_cparams(("parallel", "parallel", "arbitrary")),
        name="diff_attn",
    )(main3, main3, main3, bias, lam_rows, subln_g)


def _gelu_tanh(x):
    return 0.5 * x * (1.0 + jnp.tanh(math.sqrt(2.0 / math.pi) * (x + 0.044715 * (x * x * x))))


def _nsa_compress_kernel(rk_ref, rv_ref, pos_ref, wk1_ref, wk2_ref, wv1_ref, wv2_ref, kc_ref, vc_ref, *, n_cmp):
    half = CMP_STRIDE * HEAD_DIM
    pos = pos_ref[...]
    rows = rk_ref.shape[0]
    keep = lax.broadcasted_iota(jnp.int32, (rows, HEAD_DIM), 0) < n_cmp

    def compress(r_ref, w1_ref, w2_ref, out_ref):
        r = r_ref[...]
        first = _dot(r, w1_ref[:half, :])
        second = _dot(r, w1_ref[half:, :])
        pre = first + pltpu.roll(second, rows - 1, 0) + _dot(pos, w1_ref[...])
        out = _dot(_gelu_tanh(pre).astype(_BF16), w2_ref[...])
        out_ref[...] = jnp.where(keep, out, 0.0).astype(out_ref.dtype)

    compress(rk_ref, wk1_ref, wk2_ref, kc_ref)
    compress(rv_ref, wv1_ref, wv2_ref, vc_ref)


def _nsa_compress(rk, rv, pos_flat, wk1, wk2, wv1, wv2, n_cmp):
    b, g, rows, width = rk.shape
    blk = pl.BlockSpec((None, None, rows, width), lambda i, j: (i, j, 0, 0))
    full = lambda a: pl.BlockSpec(a.shape, lambda i, j: (0,) * a.ndim)
    out_blk = pl.BlockSpec((None, None, rows, HEAD_DIM), lambda i, j: (i, j, 0, 0))
    out_sd = jax.ShapeDtypeStruct((b, g, rows, HEAD_DIM), _BF16)
    return pl.pallas_call(
        functools.partial(_nsa_compress_kernel, n_cmp=n_cmp),
        grid=(b, g),
        in_specs=[blk, blk, full(pos_flat), full(wk1), full(wk2), full(wv1), full(wv2)],
        out_specs=[out_blk, out_blk],
        out_shape=[out_sd, out_sd],
        compiler_params=_cparams(("parallel", "parallel")),
        name="nsa_compress",
    )(rk, rv, pos_flat, wk1, wk2, wv1, wv2)


def _nsa_attn_kernel(q_ref, kc_ref, vc_ref, ks_ref, vs_ref, kw_ref, vw_ref, gate_ref, toep_ref, cmpb_ref,
                     ovt_ref, exp_ref, o_ref, *, t):
    qi = pl.program_id(2)
    t0 = qi * t
    n_sel = ovt_ref.shape[0]
    qh = [q_ref[:, h * HEAD_DIM:(h + 1) * HEAD_DIM] for h in range(NSA_HPG)]

    tpos = t0 + lax.broadcasted_iota(jnp.int32, (t, 1), 0)
    row_ok = jnp.where(tpos >= CMP_LEN - 1, 1.0, 0.0)
    kc = kc_ref[...]
    vc = vc_ref[...]
    o_cmp = []
    psum = jnp.zeros((t, LANE), _F32)
    for h in range(NSA_HPG):
        sc = _dot_nt(qh[h], kc) + cmpb_ref[h]
        e = jnp.exp2(sc - jnp.max(sc, axis=-1, keepdims=True))
        p = e / jnp.sum(e, axis=-1, keepdims=True) * row_ok
        o_cmp.append(_dot(p.astype(_BF16), vc))
        psum = psum + p

    ovt = ovt_ref[...]
    p_hi, p_mid, p_lo = _split3(psum)
    imp = (_dot_nt(ovt, p_hi) + _dot_nt(ovt, p_mid)) + _dot_nt(ovt, p_lo)
    j = lax.broadcasted_iota(jnp.int32, (n_sel, t), 0)
    blk_t = (t0 + lax.broadcasted_iota(jnp.int32, (n_sel, t), 1)) // SEL_LEN
    valid = j <= blk_t
    forced = (j == 0) | (j == blk_t) | (j == blk_t - 1)
    score = jnp.where(valid, imp + jnp.where(forced, FORCE_SCORE, 0.0), NEG_INF)
    rank = jnp.zeros((n_sel, t), _F32)
    for k in range(n_sel):
        sk = score[k:k + 1, :]
        ahead = (sk > score) | ((sk == score) & (j > k))
        rank = rank + jnp.where(ahead, 1.0, 0.0)
    drop_t = jnp.where(valid & (rank < SEL_TOPN), 0.0, NEG_INF)
    drop_t = jnp.concatenate([drop_t, jnp.zeros((LANE - n_sel, t), _F32)], axis=0)
    drop = jnp.transpose(drop_t).astype(_BF16)

    q_all = jnp.concatenate(qh, axis=0)
    rows = NSA_HPG * t

    def add_bias(s, which):
        return (s.reshape(NSA_HPG, t, t) + toep_ref[which]).reshape(rows, t)

    q_sel = jnp.concatenate([q_all, jnp.concatenate([drop] * NSA_HPG, axis=0)], axis=1)

    def sel_step(kj, carry, near):
        ks = pl.ds(pl.multiple_of(kj * t, t), t)
        s = _dot_nt(q_sel, jnp.concatenate([ks_ref[ks, :], exp_ref[ks, :]], axis=1))
        if near:
            s = add_bias(s, qi - kj)
        return _online_update(s, vs_ref[ks, :], *carry)

    n_far = jnp.maximum(qi - 1, 0)
    carry = lax.fori_loop(0, n_far, functools.partial(sel_step, near=False), _softmax_init(rows))
    _, l_s, a_s = lax.fori_loop(n_far, qi + 1, functools.partial(sel_step, near=True), carry)
    o_sel = a_s / l_s

    d_edge = WINDOW // t

    def win_body(kj, carry):
        ks = pl.ds(pl.multiple_of(kj * t, t), t)
        d = qi - kj
        s = add_bias(_dot_nt(q_all, kw_ref[ks, :]), jnp.where(d == d_edge, 2, d))
        return _online_update(s, vw_ref[ks, :], *carry)

    _, l_w, a_w = lax.fori_loop(jnp.maximum(qi - d_edge, 0), qi + 1, win_body, _softmax_init(rows))
    o_win = a_w / l_w

    gl = gate_ref[...]
    gates = 1.0 / (1.0 + jnp.exp(-gl))
    for h in range(NSA_HPG):
        out = (gates[:, 3 * h:3 * h + 1] * o_cmp[h]
               + gates[:, 3 * h + 1:3 * h + 2] * o_sel[h * t:(h + 1) * t, :]
               + gates[:, 3 * h + 2:3 * h + 3] * o_win[h * t:(h + 1) * t, :])
        o_ref[:, h * HEAD_DIM:(h + 1) * HEAD_DIM] = out.astype(o_ref.dtype)


def _nsa_attn(main3, kcmp, vcmp, gates3, toep, cmpb, ovt, expand, t=ATT_T):
    b, s, _ = main3.shape
    cb = lambda c: c // LANE
    gw = NSA_HPG * HEAD_DIM
    kv = lambda c: pl.BlockSpec((None, s, LANE), lambda g, i, q: (i, 0, cb(c) + g))
    cmp_blk = pl.BlockSpec((None, None, LANE, HEAD_DIM), lambda g, i, q: (i, g, 0, 0))
    return pl.pallas_call(
        functools.partial(_nsa_attn_kernel, t=t),
        grid=(NSA_KV_GROUPS, b, s // t),
        in_specs=[
            pl.BlockSpec((None, t, gw), lambda g, i, q: (i, q, C_NQ // gw + g)),
            cmp_blk, cmp_blk,
            kv(C_NKS), kv(C_NVS), kv(C_NKW), kv(C_NVW),
            pl.BlockSpec((None, t, LANE), lambda g, i, q: (i, q, 1 + g)),
            pl.BlockSpec((None, 3, NSA_HPG, t, t), lambda g, i, q: (g, 0, 0, 0, 0)),
            pl.BlockSpec((NSA_HPG, t, LANE), lambda g, i, q: (g, q, 0)),
            pl.BlockSpec(ovt.shape, lambda g, i, q: (0, 0)),
            pl.BlockSpec(expand.shape, lambda g, i, q: (0, 0)),
        ],
        out_specs=pl.BlockSpec((None, t, gw), lambda g, i, q: (i, q, g)),
        out_shape=jax.ShapeDtypeStruct((b, s, NSA_HEADS * HEAD_DIM), _BF16),
        compiler_params=_cparams(("parallel", "parallel", "arbitrary")),
        name="nsa_attn",
    )(main3, kcmp, vcmp, main3, main3, main3, main3, gates3, toep, cmpb, ovt, expand)


def _outproj_kernel(a1_ref, a2_ref, a3_ref, w1_ref, w2_ref, w3_ref, h_ref, o_ref):
    o_ref[...] = h_ref[...] + ((_dot(a1_ref[...], w1_ref[...]) + _dot(a2_ref[...], w2_ref[...]))
                               + _dot(a3_ref[...], w3_ref[...]))


def _outproj(a1, a2, a3, w1, w2, w3, h, tm=512):
    m = h.shape[0]
    act = lambda a: pl.BlockSpec((tm, a.shape[1]), lambda i: (i, 0))
    full = lambda a: pl.BlockSpec(a.shape, lambda i: (0, 0))
    return pl.pallas_call(
        _outproj_kernel,
        grid=(m // tm,),
        in_specs=[act(a1), act(a2), act(a3), full(w1), full(w2), full(w3), act(h)],
        out_specs=act(h),
        out_shape=jax.ShapeDtypeStruct(h.shape, _F32),
        compiler_params=_cparams(("parallel",)),
        name="outproj_residual",
    )(a1, a2, a3, w1, w2, w3, h)


def _ffn_kernel(x_ref, halo_ref, g_ref, wg_ref, wu_ref, cwg_ref, cwu_ref, cbg_ref, cbu_ref, wd_ref, o_ref,
                xn_ref, acc_ref, hg_ref, hu_ref, *, tm, seq):
    i = pl.program_id(0)
    j = pl.program_id(1)

    @pl.when(j == 0)
    def _():
        g = g_ref[...]
        xn_ref[HALO:, :] = _rms(x_ref[...], g).astype(_BF16)
        keep = jnp.where((i * tm) % seq == 0, 0.0, 1.0)
        xn_ref[:HALO, :] = (_rms(halo_ref[0], g) * keep).astype(_BF16)
        acc_ref[...] = jnp.zeros_like(acc_ref)

    xn = xn_ref[...]
    hg_ref[...] = _dot(xn, wg_ref[...])
    hu_ref[...] = _dot(xn, wu_ref[...])

    def conv(h_ref, cw_ref, cb_ref):
        out = cb_ref[...] + cw_ref[CONV_W - 1:CONV_W, :] * h_ref[HALO:, :]
        for tap in range(CONV_W - 1):
            shift = CONV_W - 1 - tap
            out = out + cw_ref[tap:tap + 1, :] * h_ref[HALO - shift:HALO - shift + tm, :]
        return out

    gate = conv(hg_ref, cwg_ref, cbg_ref)
    up = conv(hu_ref, cwu_ref, cbu_ref)
    act = gate / (1.0 + jnp.exp(-gate)) * up
    acc_ref[...] += _dot(act.astype(_BF16), wd_ref[...])

    @pl.when(j == pl.num_programs(1) - 1)
    def _():
        o_ref[...] = x_ref[...] + acc_ref[...]


def _ffn(h, g, w_up, conv_w, conv_b, w_down, seq, tm=512, tf=512):
    m = h.shape[0]
    nf = D_FF // tf
    halo_view = h.reshape(m // HALO, HALO, D_MODEL)
    return pl.pallas_call(
        functools.partial(_ffn_kernel, tm=tm, seq=seq),
        grid=(m // tm, nf),
        in_specs=[
            pl.BlockSpec((tm, D_MODEL), lambda i, j: (i, 0)),
            pl.BlockSpec((1, HALO, D_MODEL), lambda i, j: (jnp.maximum(i * (tm // HALO) - 1, 0), 0, 0)),
            pl.BlockSpec((1, D_MODEL), lambda i, j: (0, 0)),
            pl.BlockSpec((D_MODEL, tf), lambda i, j: (0, j)),
            pl.BlockSpec((D_MODEL, tf), lambda i, j: (0, j + nf)),
            pl.BlockSpec((CONV_W, tf), lambda i, j: (0, j)),
            pl.BlockSpec((CONV_W, tf), lambda i, j: (0, j + nf)),
            pl.BlockSpec((1, tf), lambda i, j: (0, j)),
            pl.BlockSpec((1, tf), lambda i, j: (0, j + nf)),
            pl.BlockSpec((tf, D_MODEL), lambda i, j: (j, 0)),
        ],
        out_specs=pl.BlockSpec((tm, D_MODEL), lambda i, j: (i, 0)),
        out_shape=jax.ShapeDtypeStruct(h.shape, _F32),
        scratch_shapes=[
            pltpu.VMEM((tm + HALO, D_MODEL), _BF16),
            pltpu.VMEM((tm, D_MODEL), _F32),
            pltpu.VMEM((tm + HALO, tf), _F32),
            pltpu.VMEM((tm + HALO, tf), _F32),
        ],
        compiler_params=_cparams(("parallel", "arbitrary")),
        name="rms_conv_ffn",
    )(h, halo_view, g, w_up, w_up, conv_w, conv_w, conv_b, conv_b, w_down)


def _final_norm_kernel(x_ref, g_ref, o_ref):
    o_ref[...] = _rms(x_ref[...], g_ref[...])


def _final_norm(h, g, tm=512):
    m = h.shape[0]
    return pl.pallas_call(
        _final_norm_kernel,
        grid=(m // tm,),
        in_specs=[pl.BlockSpec((tm, D_MODEL), lambda i: (i, 0)), pl.BlockSpec((1, D_MODEL), lambda i: (0, 0))],
        out_specs=pl.BlockSpec((tm, D_MODEL), lambda i: (i, 0)),
        out_shape=jax.ShapeDtypeStruct(h.shape, _F32),
        compiler_params=_cparams(("parallel",)),
        name="final_rmsnorm",
    )(h, g)


def _pad_lanes(v, width=LANE):
    return jnp.pad(v, ((0, 0), (0, width - v.shape[-1])))


def kernel(x, attn_norm_g, w_in, fox_f_bias, diff_lq1, diff_lk1, diff_lq2, diff_lk2, diff_subln_g, nsa_cmp_pos, nsa_cmp_wk1, nsa_cmp_wk2, nsa_cmp_wv1, nsa_cmp_wv2, w_out, ffn_norm_g, ffn_w_up, ffn_conv_w, ffn_conv_b, ffn_w_down, rel_bias, final_norm_g):
    b, s, _ = x.shape
    depth = w_in.shape[0]
    m = b * s
    t = ATT_T
    n_cmp = (s - CMP_LEN) // CMP_STRIDE + 1
    n_sel = s // SEL_LEN
    assert n_cmp + 1 == s // CMP_STRIDE == LANE and n_sel >= SEL_TOPN and s % t == 0

    gate_idx, gate_valid = _gate_perm()
    w_main = jnp.take(w_in, jnp.asarray(_main_perm()), axis=2).astype(_BF16)
    w_gate = (jnp.take(w_in, jnp.asarray(gate_idx), axis=2) * jnp.asarray(gate_valid)).astype(_BF16)
    w_out_b = w_out.astype(_BF16)
    w_up_b = ffn_w_up.astype(_BF16)
    w_down_b = ffn_w_down.astype(_BF16)
    wk1, wk2 = nsa_cmp_wk1.astype(_BF16), nsa_cmp_wk2.astype(_BF16)
    wv1, wv2 = nsa_cmp_wv1.astype(_BF16), nsa_cmp_wv2.astype(_BF16)
    pos_flat = nsa_cmp_pos.reshape(depth, 1, CMP_LEN * HEAD_DIM).astype(_BF16)
    f_bias_rows = _pad_lanes(fox_f_bias)
    lam_rows = jnp.stack([_pad_lanes(v) for v in (diff_lq1, diff_lk1, diff_lq2, diff_lk2)], axis=1)
    lam_rows = jnp.pad(lam_rows, ((0, 0), (0, 4), (0, 0)))

    toep_idx = _toeplitz_bucket_tiles(t).reshape(3 * t, t)
    diff_bias = _bias_tables(rel_bias, toep_idx, DIFF_HEADS, 0).reshape(DIFF_HEADS, 3, t, t)
    nsa_bias = _bias_tables(rel_bias, toep_idx, NSA_HEADS, DIFF_HEADS).reshape(NSA_KV_GROUPS, NSA_HPG, 3, t, t)
    nsa_bias = jnp.transpose(nsa_bias, (0, 2, 1, 3, 4))
    cmp_bias = _bias_tables(rel_bias, _cmp_bucket_tile(s), NSA_HEADS, DIFF_HEADS)
    cmp_starts = np.arange(LANE) * CMP_STRIDE
    sel_starts = np.arange(n_sel) * SEL_LEN
    overlap = np.clip(np.minimum(cmp_starts[:, None] + CMP_LEN, sel_starts[None, :] + SEL_LEN)
                      - np.maximum(cmp_starts[:, None], sel_starts[None, :]), 0, None).astype(np.float32) / CMP_LEN
    overlap[n_cmp:] = 0.0
    ovt = jnp.asarray(overlap.T, _BF16)
    expand = np.zeros((s, LANE), np.float32)
    expand[np.arange(s), np.arange(s) // SEL_LEN] = 1.0
    expand = jnp.asarray(expand, _BF16)

    def cmp_rows(main3, c):
        a = main3[:, :, c:c + NSA_KV_GROUPS * HEAD_DIM]
        a = a.reshape(b, s // CMP_STRIDE, CMP_STRIDE, NSA_KV_GROUPS, HEAD_DIM)
        return jnp.transpose(a, (0, 3, 1, 2, 4)).reshape(b, NSA_KV_GROUPS, s // CMP_STRIDE, CMP_STRIDE * HEAD_DIM)

    h = x.reshape(m, D_MODEL)
    for l in range(depth):
        main, gates = _inproj(h, attn_norm_g[l][None], w_main[l], w_gate[l])
        main3 = main.reshape(b, s, N_MAIN)
        gates3 = gates.reshape(b, s, N_GATE)

        qa, ka = _fox_prep(gates3, f_bias_rows[l][None])
        fox_o = _fox_attn(main3, qa, ka)

        lam_init = 0.8 - 0.6 * math.exp(-0.3 * l)
        diff_o = _diff_attn(main3, diff_bias, lam_rows[l], diff_subln_g[l][None], lam_init)

        kcmp, vcmp = _nsa_compress(cmp_rows(main3, C_NKC), cmp_rows(main3, C_NVC), pos_flat[l],
                                   wk1[l], wk2[l], wv1[l], wv2[l], n_cmp)
        nsa_o = _nsa_attn(main3, kcmp, vcmp, gates3, nsa_bias, cmp_bias, ovt, expand)

        wo = w_out_b[l]
        h = _outproj(fox_o.reshape(m, -1), diff_o.reshape(m, -1), nsa_o.reshape(m, -1),
                     wo[:768], wo[768:1280], wo[1280:], h)
        h = _ffn(h, ffn_norm_g[l][None], w_up_b[l], ffn_conv_w[l], ffn_conv_b[l][None], w_down_b[l], s)
    return _final_norm(h, final_norm_g[None]).reshape(b, s, D_MODEL)
```

```python
import functools
import math

import jax
import jax.numpy as jnp
import numpy as np
from jax import lax
from jax.experimental import pallas as pl
from jax.experimental.pallas import tpu as pltpu

D_MODEL = 2048
HEAD_DIM = 128
FOX_HEADS = 6
DIFF_HEADS = 4
DIFF_QK_DIM = HEAD_DIM // 2
NSA_HEADS = 6
NSA_KV_GROUPS = 2
NSA_HPG = NSA_HEADS // NSA_KV_GROUPS
CMP_LEN = 32
CMP_STRIDE = 16
SEL_LEN = 64
SEL_TOPN = 8
WINDOW = 512
N_BUCKETS = 32
MAX_DISTANCE = 128
D_FF = 5632
CONV_W = 3
EPS = 1e-6
NEG_INF = -1e30
FORCE_SCORE = 1e4

IN_SIZES = (
    FOX_HEADS * HEAD_DIM, FOX_HEADS * HEAD_DIM, FOX_HEADS * HEAD_DIM, FOX_HEADS,
    DIFF_HEADS * 2 * DIFF_QK_DIM, DIFF_HEADS * 2 * DIFF_QK_DIM, DIFF_HEADS * HEAD_DIM,
    NSA_HEADS * HEAD_DIM,
    NSA_KV_GROUPS * HEAD_DIM, NSA_KV_GROUPS * HEAD_DIM,
    NSA_KV_GROUPS * HEAD_DIM, NSA_KV_GROUPS * HEAD_DIM,
    NSA_KV_GROUPS * HEAD_DIM, NSA_KV_GROUPS * HEAD_DIM,
    NSA_HEADS * 3,
)
_IN_OFF = np.concatenate([[0], np.cumsum(IN_SIZES)])
(_O_FQ, _O_FK, _O_FV, _O_FF, _O_DQ, _O_DK, _O_DV, _O_NQ, _O_NKC, _O_NVC, _O_NKS, _O_NVS,
 _O_NKW, _O_NVW, _O_NG) = [int(v) for v in _IN_OFF[:-1]]

LANE = 128
HALO = 8
C_FQ, C_FK, C_FV, C_NQ = 0, 768, 1536, 2304
C_DQ, C_DK, C_DV = 3072, 3584, 4096
C_NKC, C_NVC, C_NKS, C_NVS, C_NKW, C_NVW = 4608, 4864, 5120, 5376, 5632, 5888
N_MAIN = 6144
CHUNK_W = 768
N_CHUNK = N_MAIN // CHUNK_W
N_GATE = 3 * LANE
ATT_T = 512
LOG2E = math.log2(math.e)
VMEM_LIMIT = 56 * 1024 * 1024

_F32 = jnp.float32
_BF16 = jnp.bfloat16


_MAIN_SEGMENTS = ((_O_FQ, 768), (_O_FK, 768), (_O_FV, 768), (_O_NQ, 768), (_O_DQ, 512), (_O_DK, 512),
                  (_O_DV, 512), (_O_NKC, 256), (_O_NVC, 256), (_O_NKS, 256), (_O_NVS, 256),
                  (_O_NKW, 256), (_O_NVW, 256))


def _t5_bucket_np(rel):
    n = np.maximum(rel, 0)
    max_exact = N_BUCKETS // 2
    nf = np.maximum(n, 1).astype(np.float32)
    large = max_exact + (np.log(nf / np.float32(max_exact)) / np.float32(math.log(MAX_DISTANCE / max_exact))
                         * np.float32(N_BUCKETS - max_exact)).astype(np.int32)
    large = np.minimum(large, N_BUCKETS - 1)
    return np.where(n < max_exact, n, large).astype(np.int32)


def _toeplitz_bucket_tiles(t):
    i = np.arange(t)[:, None]
    j = np.arange(t)[None, :]
    d0 = np.where(j <= i, _t5_bucket_np(i - j), -1)
    d1 = _t5_bucket_np(i - j + t)
    assert np.all(_t5_bucket_np(np.arange(t + 1, 8 * t)) == N_BUCKETS - 1)
    assert WINDOW in (t, 2 * t)
    edge = np.where(i < j, _t5_bucket_np(i - j + WINDOW), -1)
    return np.stack([d0, d1, edge]).astype(np.int32)


def _cmp_bucket_tile(s):
    n = np.arange(LANE)[None, :]
    t = np.arange(s)[:, None]
    n_cmp = (s - CMP_LEN) // CMP_STRIDE + 1
    block_end = n * CMP_STRIDE + CMP_LEN - 1
    ok = (block_end <= t) & (n < n_cmp)
    return np.where(ok, _t5_bucket_np(t - block_end), -1).astype(np.int32)


def _dot(a, b):
    return jnp.dot(a, b, preferred_element_type=_F32)


def _dot_nt(a, b):
    return lax.dot_general(a, b, (((1,), (1,)), ((), ())), preferred_element_type=_F32)


def _split3(x):
    hi = x.astype(_BF16)
    r = x - hi.astype(_F32)
    mid = r.astype(_BF16)
    lo = (r - mid.astype(_F32)).astype(_BF16)
    return hi, mid, lo


def _rms(x, g):
    return x * lax.rsqrt(jnp.mean(x * x, axis=-1, keepdims=True) + EPS) * g


def _cparams(sem, vmem=VMEM_LIMIT):
    return pltpu.CompilerParams(dimension_semantics=sem, vmem_limit_bytes=vmem)


def _cols(rows, col, width, where):
    per = CHUNK_W // width

    def index_map(*grid):
        batch, row_blk, off = where(*grid)
        blk = col // width + off
        return blk // per, batch, row_blk, blk % per

    return pl.BlockSpec((None, None, rows, width), index_map)


def _inproj_kernel(x_ref, g_ref, w_ref, wg_ref, cs_ref, main_ref, gate_ref, xn_ref):
    xn_ref[...] = _rms(x_ref[...], g_ref[...]).astype(_BF16)
    gate_ref[...] = _dot(xn_ref[...], wg_ref[...])

    def chunk(c, carry):
        main_ref[c] = (_dot(xn_ref[...], w_ref[c]) * cs_ref[c]).astype(_BF16)
        return carry

    lax.fori_loop(0, N_CHUNK, chunk, 0)


def _query_col_scale():
    cs = np.ones((1, N_MAIN), np.float32)
    cs[:, C_FQ:C_FQ + FOX_HEADS * HEAD_DIM] = HEAD_DIM ** -0.5 * LOG2E
    cs[:, C_NQ:C_NQ + NSA_HEADS * HEAD_DIM] = HEAD_DIM ** -0.5 * LOG2E
    cs[:, C_DQ:C_DQ + DIFF_HEADS * 2 * DIFF_QK_DIM] = DIFF_QK_DIM ** -0.5 * LOG2E
    return cs.reshape(N_CHUNK, 1, CHUNK_W)


def _inproj(h, g, w_main, w_gate, layer, tm=256):
    m = h.shape[0]
    once = pl.Buffered(1)
    return pl.pallas_call(
        _inproj_kernel,
        grid=(m // tm,),
        in_specs=[
            pl.BlockSpec((tm, D_MODEL), lambda i: (i, 0)),
            pl.BlockSpec((None, 1, D_MODEL), lambda i: (layer, 0, 0)),
            pl.BlockSpec((None, N_CHUNK, D_MODEL, CHUNK_W), lambda i: (layer, 0, 0, 0), pipeline_mode=once),
            pl.BlockSpec((None, D_MODEL, N_GATE), lambda i: (layer, 0, 0), pipeline_mode=once),
            pl.BlockSpec((N_CHUNK, 1, CHUNK_W), lambda i: (0, 0, 0), pipeline_mode=once),
        ],
        out_specs=[
            pl.BlockSpec((N_CHUNK, tm, CHUNK_W), lambda i: (0, i, 0)),
            pl.BlockSpec((tm, N_GATE), lambda i: (i, 0)),
        ],
        out_shape=[jax.ShapeDtypeStruct((N_CHUNK, m, CHUNK_W), _BF16), jax.ShapeDtypeStruct((m, N_GATE), _F32)],
        scratch_shapes=[pltpu.VMEM((tm, D_MODEL), _BF16)],
        compiler_params=_cparams(("parallel",)),
        name="rms_inproj",
    )(h, g, w_main, w_gate, jnp.asarray(_query_col_scale()))


def _bias_table_kernel(rel_ref, idx_ref, out_ref):
    h = pl.program_id(0)
    idx = idx_ref[...]
    acc = jnp.full(idx.shape, NEG_INF, _F32)
    last = rel_ref[N_BUCKETS - 1, h]
    for b in range(N_BUCKETS):
        acc = jnp.where(idx == b, (rel_ref[b, h] - last) * LOG2E, acc)
    out_ref[...] = acc


def _bias_tables(rel_bias, idx, n_heads, head0):
    r, c = idx.shape
    return pl.pallas_call(
        _bias_table_kernel,
        grid=(n_heads,),
        in_specs=[
            pl.BlockSpec(memory_space=pltpu.SMEM),
            pl.BlockSpec((r, c), lambda h: (0, 0)),
        ],
        out_specs=pl.BlockSpec((None, r, c), lambda h: (h, 0, 0)),
        out_shape=jax.ShapeDtypeStruct((n_heads, r, c), _F32),
        compiler_params=_cparams(("arbitrary",)),
        name="t5_bias_tables",
    )(rel_bias[:, head0:head0 + n_heads], jnp.asarray(idx))


def _fox_prep_kernel(gate_ref, fb_ref, qa_ref, ka_ref, *, chunk):
    s = gate_ref.shape[0]
    row = lax.broadcasted_iota(jnp.int32, (chunk, chunk), 0)
    col = lax.broadcasted_iota(jnp.int32, (chunk, chunk), 1)
    tri = jnp.where(col <= row, 1.0, 0.0).astype(_BF16)
    lane = lax.broadcasted_iota(jnp.int32, (chunk, LANE), 1)
    carry = jnp.zeros((1, LANE), _F32)
    for c in range(s // chunk):
        x = gate_ref[c * chunk:(c + 1) * chunk, :] + fb_ref[...]
        log_f = jnp.minimum(x, 0.0) - jnp.log1p(jnp.exp(-jnp.abs(x)))
        hi, mid, lo = _split3(log_f)
        cum = (_dot(tri, hi) + _dot(tri, mid)) + _dot(tri, lo) + carry
        carry = cum[chunk - 1:chunk, :]
        for h in range(FOX_HEADS):
            chi, cmid, clo = [p.astype(_F32) for p in _split3(cum[:, h:h + 1] * LOG2E)]
            qa = jnp.where(lane == 0, chi, jnp.where(lane == 1, cmid, jnp.where(lane == 2, clo,
                           jnp.where(lane < 6, 1.0, 0.0))))
            ka = jnp.where(lane < 3, 1.0, jnp.where(lane == 3, -chi, jnp.where(lane == 4, -cmid,
                           jnp.where(lane == 5, -clo, 0.0))))
            qa_ref[c * chunk:(c + 1) * chunk, h * LANE:(h + 1) * LANE] = qa.astype(_BF16)
            ka_ref[c * chunk:(c + 1) * chunk, h * LANE:(h + 1) * LANE] = ka.astype(_BF16)


def _fox_prep(gates3, f_bias_rows, layer):
    b, s, _ = gates3.shape
    w = FOX_HEADS * LANE
    return pl.pallas_call(
        functools.partial(_fox_prep_kernel, chunk=256),
        grid=(b,),
        in_specs=[
            pl.BlockSpec((None, s, LANE), lambda i: (i, 0, 0)),
            pl.BlockSpec((None, 1, LANE), lambda i: (layer, 0, 0)),
        ],
        out_specs=[
            pl.BlockSpec((None, s, w), lambda i: (i, 0, 0)),
            pl.BlockSpec((None, s, w), lambda i: (i, 0, 0)),
        ],
        out_shape=[jax.ShapeDtypeStruct((b, s, w), _BF16), jax.ShapeDtypeStruct((b, s, w), _BF16)],
        compiler_params=_cparams(("parallel",)),
        name="fox_prep",
    )(gates3, f_bias_rows)


def _online_update(s, v, m, l, acc):
    m_new = jnp.maximum(m, jnp.max(s, axis=-1, keepdims=True))
    alpha = jnp.exp2(m - m_new)
    p = jnp.exp2(s - m_new)
    l_new = alpha * l + jnp.sum(p, axis=-1, keepdims=True)
    acc_new = alpha * acc + _dot(p.astype(_BF16), v)
    return m_new, l_new, acc_new


def _softmax_init(rows):
    return (jnp.full((rows, 1), NEG_INF, _F32), jnp.zeros((rows, 1), _F32), jnp.zeros((rows, HEAD_DIM), _F32))


def _fox_attn_kernel(q_ref, k_ref, v_ref, qa_ref, ka_ref, o_ref, *, t):
    qi = pl.program_id(2)
    q = jnp.concatenate([q_ref[...], qa_ref[...]], axis=1)

    def scores(kj):
        ks = pl.ds(pl.multiple_of(kj * t, t), t)
        return _dot_nt(q, jnp.concatenate([k_ref[ks, :], ka_ref[ks, :]], axis=1)), v_ref[ks, :]

    def body(kj, carry):
        s, v = scores(kj)
        return _online_update(s, v, *carry)

    carry = lax.fori_loop(0, qi, body, _softmax_init(t))
    s, v = scores(qi)
    row = lax.broadcasted_iota(jnp.int32, (t, t), 0)
    col = lax.broadcasted_iota(jnp.int32, (t, t), 1)
    s = jnp.where(col <= row, s, NEG_INF)
    _, l, acc = _online_update(s, v, *carry)
    o_ref[...] = (acc / l).astype(o_ref.dtype)


def _fox_attn(main4, qa, ka, t=ATT_T):
    _, b, s, _ = main4.shape
    return pl.pallas_call(
        functools.partial(_fox_attn_kernel, t=t),
        grid=(b, FOX_HEADS, s // t),
        in_specs=[
            _cols(t, C_FQ, LANE, lambda i, h, q: (i, q, h)),
            _cols(s, C_FK, LANE, lambda i, h, q: (i, 0, h)),
            _cols(s, C_FV, LANE, lambda i, h, q: (i, 0, h)),
            pl.BlockSpec((None, t, LANE), lambda i, h, q: (i, q, h)),
            pl.BlockSpec((None, s, LANE), lambda i, h, q: (i, 0, h)),
        ],
        out_specs=pl.BlockSpec((None, t, LANE), lambda i, h, q: (i, q, h)),
        out_shape=jax.ShapeDtypeStruct((b, s, FOX_HEADS * HEAD_DIM), _BF16),
        compiler_params=_cparams(("parallel", "parallel", "arbitrary")),
        name="fox_attn",
    )(main4, main4, main4, qa, ka)


def _diff_attn_kernel(q_ref, k_ref, v_ref, bias_ref, lam_ref, g_ref, o_ref, *, t, lam_init):
    qi = pl.program_id(2)
    q = q_ref[...]
    lane = lax.broadcasted_iota(jnp.int32, (1, LANE), 1)
    first_map = jnp.where(lane < DIFF_QK_DIM, 1.0, 0.0).astype(q.dtype)
    q1 = q * first_map
    q_both = jnp.concatenate([q1, q - q1], axis=0)

    def step(kj, carry, near):
        ks = pl.ds(pl.multiple_of(kj * t, t), t)
        s = _dot_nt(q_both, k_ref[ks, :])
        if near:
            s = (s.reshape(2, t, t) + bias_ref[qi - kj][None]).reshape(2 * t, t)
        return _online_update(s, v_ref[ks, :], *carry)

    n_far = jnp.maximum(qi - 1, 0)
    carry = lax.fori_loop(0, n_far, functools.partial(step, near=False), _softmax_init(2 * t))
    _, l, a = lax.fori_loop(n_far, qi + 1, functools.partial(step, near=True), carry)
    lv = lam_ref[...]
    lam = (jnp.exp(jnp.sum(lv[0:1, :] * lv[1:2, :], axis=-1, keepdims=True))
           - jnp.exp(jnp.sum(lv[2:3, :] * lv[3:4, :], axis=-1, keepdims=True)) + lam_init)
    o = a[:t] / l[:t] - lam * (a[t:] / l[t:])
    o_ref[...] = (_rms(o, g_ref[...]) * (1.0 - lam_init)).astype(o_ref.dtype)


def _diff_attn(main4, bias, lam_rows, subln_g, layer, lam_init, t=ATT_T):
    _, b, s, _ = main4.shape
    return pl.pallas_call(
        functools.partial(_diff_attn_kernel, t=t, lam_init=lam_init),
        grid=(DIFF_HEADS, b, s // t),
        in_specs=[
            _cols(t, C_DQ, LANE, lambda h, i, q: (i, q, h)),
            _cols(s, C_DK, LANE, lambda h, i, q: (i, 0, h)),
            _cols(s, C_DV, LANE, lambda h, i, q: (i, 0, h)),
            pl.BlockSpec((None, 3, t, t), lambda h, i, q: (h, 0, 0, 0)),
            pl.BlockSpec((None, 8, LANE), lambda h, i, q: (layer, 0, 0)),
            pl.BlockSpec((None, 1, LANE), lambda h, i, q: (layer, 0, 0)),
        ],
        out_specs=pl.BlockSpec((None, t, LANE), lambda h, i, q: (i, q, h)),
        out_shape=jax.ShapeDtypeStruct((b, s, DIFF_HEADS * HEAD_DIM), _BF16),
        compiler_params=_cparams(("parallel", "parallel", "arbitrary")),
        name="diff_attn",
    )(main4, main4, main4, bias, lam_rows, subln_g)


def _gelu_tanh(x):
    return 0.5 * x * (1.0 + jnp.tanh(math.sqrt(2.0 / math.pi) * (x + 0.044715 * (x * x * x))))


def _nsa_compress_kernel(k_ref, v_ref, pos_ref, wk1_ref, wk2_ref, wv1_ref, wv2_ref, kc_ref, vc_ref,
                         stage_ref, rows_ref, *, n_cmp):
    half = CMP_STRIDE * HEAD_DIM
    pos = pos_ref[...]
    rows = rows_ref.shape[0]
    keep = lax.broadcasted_iota(jnp.int32, (rows, HEAD_DIM), 0) < n_cmp

    def compress(x_ref, w1_ref, w2_ref, out_ref):
        stage_ref[...] = x_ref[...].astype(_F32)
        for l in range(CMP_STRIDE):
            rows_ref[:, l * HEAD_DIM:(l + 1) * HEAD_DIM] = (
                stage_ref[pl.ds(l, rows, stride=CMP_STRIDE), :].astype(_BF16))
        r = rows_ref[...]
        first = _dot(r, w1_ref[:half, :])
        second = _dot(r, w1_ref[half:, :])
        pre = first + pltpu.roll(second, rows - 1, 0) + _dot(pos, w1_ref[...])
        out = _dot(_gelu_tanh(pre).astype(_BF16), w2_ref[...])
        out_ref[...] = jnp.where(keep, out, 0.0).astype(out_ref.dtype)

    compress(k_ref, wk1_ref, wk2_ref, kc_ref)
    compress(v_ref, wv1_ref, wv2_ref, vc_ref)


def _nsa_compress(main4, pos_flat, wk1, wk2, wv1, wv2, layer, n_cmp):
    _, b, s, _ = main4.shape
    rows = s // CMP_STRIDE
    lay = lambda a: pl.BlockSpec((None,) + a.shape[1:], lambda i, j: (layer,) + (0,) * (a.ndim - 1))
    out_blk = pl.BlockSpec((None, None, rows, HEAD_DIM), lambda i, j: (i, j, 0, 0))
    out_sd = jax.ShapeDtypeStruct((b, NSA_KV_GROUPS, rows, HEAD_DIM), _BF16)
    return pl.pallas_call(
        functools.partial(_nsa_compress_kernel, n_cmp=n_cmp),
        grid=(b, NSA_KV_GROUPS),
        in_specs=[
            _cols(s, C_NKC, LANE, lambda i, j: (i, 0, j)),
            _cols(s, C_NVC, LANE, lambda i, j: (i, 0, j)),
            lay(pos_flat), lay(wk1), lay(wk2), lay(wv1), lay(wv2),
        ],
        out_specs=[out_blk, out_blk],
        out_shape=[out_sd, out_sd],
        scratch_shapes=[pltpu.VMEM((s, HEAD_DIM), _F32), pltpu.VMEM((rows, CMP_STRIDE * HEAD_DIM), _BF16)],
        compiler_params=_cparams(("parallel", "parallel")),
        name="nsa_compress",
    )(main4, main4, pos_flat, wk1, wk2, wv1, wv2)


def _nsa_attn_kernel(q_ref, kc_ref, vc_ref, ks_ref, vs_ref, kw_ref, vw_ref, gate_ref, toep_ref, cmpb_ref,
                     ovt_ref, exp_ref, o_ref, *, t):
    qi = pl.program_id(2)
    t0 = qi * t
    n_sel = ovt_ref.shape[0]
    qh = [q_ref[:, h * HEAD_DIM:(h + 1) * HEAD_DIM] for h in range(NSA_HPG)]

    tpos = t0 + lax.broadcasted_iota(jnp.int32, (t, 1), 0)
    row_ok = jnp.where(tpos >= CMP_LEN - 1, 1.0, 0.0)
    kc = kc_ref[...]
    vc = vc_ref[...]
    o_cmp = []
    psum = jnp.zeros((t, LANE), _F32)
    for h in range(NSA_HPG):
        sc = _dot_nt(qh[h], kc) + cmpb_ref[h]
        e = jnp.exp2(sc - jnp.max(sc, axis=-1, keepdims=True))
        p = e / jnp.sum(e, axis=-1, keepdims=True) * row_ok
        o_cmp.append(_dot(p.astype(_BF16), vc))
        psum = psum + p

    ovt = ovt_ref[...]
    p_hi, p_mid, p_lo = _split3(psum)
    imp = (_dot_nt(ovt, p_hi) + _dot_nt(ovt, p_mid)) + _dot_nt(ovt, p_lo)
    j = lax.broadcasted_iota(jnp.int32, (n_sel, t), 0)
    blk_t = (t0 + lax.broadcasted_iota(jnp.int32, (n_sel, t), 1)) // SEL_LEN
    valid = j <= blk_t
    forced = (j == 0) | (j == blk_t) | (j == blk_t - 1)
    score = jnp.where(valid, imp + jnp.where(forced, FORCE_SCORE, 0.0), NEG_INF)
    rank = jnp.zeros((n_sel, t), _F32)
    for k in range(n_sel):
        sk = score[k:k + 1, :]
        ahead = (sk > score) | ((sk == score) & (j > k))
        rank = rank + jnp.where(ahead, 1.0, 0.0)
    drop_t = jnp.where(valid & (rank < SEL_TOPN), 0.0, NEG_INF)
    drop_t = jnp.concatenate([drop_t, jnp.zeros((LANE - n_sel, t), _F32)], axis=0)
    drop = jnp.transpose(drop_t).astype(_BF16)

    q_all = jnp.concatenate(qh, axis=0)
    rows = NSA_HPG * t

    def add_bias(s, which):
        return (s.reshape(NSA_HPG, t, t) + toep_ref[which]).reshape(rows, t)

    q_sel = jnp.concatenate([q_all, jnp.concatenate([drop] * NSA_HPG, axis=0)], axis=1)

    def sel_step(kj, carry, near):
        ks = pl.ds(pl.multiple_of(kj * t, t), t)
        s = _dot_nt(q_sel, jnp.concatenate([ks_ref[ks, :], exp_ref[ks, :]], axis=1))
        if near:
            s = add_bias(s, qi - kj)
        return _online_update(s, vs_ref[ks, :], *carry)

    n_far = jnp.maximum(qi - 1, 0)
    carry = lax.fori_loop(0, n_far, functools.partial(sel_step, near=False), _softmax_init(rows))
    _, l_s, a_s = lax.fori_loop(n_far, qi + 1, functools.partial(sel_step, near=True), carry)

    d_edge = WINDOW // t

    def win_body(kj, carry):
        ks = pl.ds(pl.multiple_of(kj * t, t), t)
        d = qi - kj
        s = add_bias(_dot_nt(q_all, kw_ref[ks, :]), jnp.where(d == d_edge, 2, d))
        return _online_update(s, vw_ref[ks, :], *carry)

    _, l_w, a_w = lax.fori_loop(jnp.maximum(qi - d_edge, 0), qi + 1, win_body, _softmax_init(rows))

    gl = gate_ref[...]
    gates = 1.0 / (1.0 + jnp.exp(-gl))
    for h in range(NSA_HPG):
        hs = slice(h * t, (h + 1) * t)
        out = (gates[:, 3 * h:3 * h + 1] * o_cmp[h]
               + (gates[:, 3 * h + 1:3 * h + 2] / l_s[hs]) * a_s[hs]
               + (gates[:, 3 * h + 2:3 * h + 3] / l_w[hs]) * a_w[hs])
        o_ref[:, h * HEAD_DIM:(h + 1) * HEAD_DIM] = out.astype(o_ref.dtype)


def _nsa_attn(main4, kcmp, vcmp, gates3, toep, cmpb, ovt, expand, t=ATT_T):
    _, b, s, _ = main4.shape
    gw = NSA_HPG * HEAD_DIM
    kv = lambda c: _cols(s, c, LANE, lambda g, i, q: (i, 0, g))
    cmp_blk = pl.BlockSpec((None, None, LANE, HEAD_DIM), lambda g, i, q: (i, g, 0, 0))
    return pl.pallas_call(
        functools.partial(_nsa_attn_kernel, t=t),
        grid=(NSA_KV_GROUPS, b, s // t),
        in_specs=[
            _cols(t, C_NQ, gw, lambda g, i, q: (i, q, g)),
            cmp_blk, cmp_blk,
            kv(C_NKS), kv(C_NVS), kv(C_NKW), kv(C_NVW),
            pl.BlockSpec((None, t, LANE), lambda g, i, q: (i, q, 1 + g)),
            pl.BlockSpec((None, 3, NSA_HPG, t, t), lambda g, i, q: (g, 0, 0, 0, 0)),
            pl.BlockSpec((NSA_HPG, t, LANE), lambda g, i, q: (g, q, 0)),
            pl.BlockSpec(ovt.shape, lambda g, i, q: (0, 0)),
            pl.BlockSpec(expand.shape, lambda g, i, q: (0, 0)),
        ],
        out_specs=pl.BlockSpec((None, t, gw), lambda g, i, q: (i, q, g)),
        out_shape=jax.ShapeDtypeStruct((b, s, NSA_HEADS * HEAD_DIM), _BF16),
        compiler_params=_cparams(("parallel", "parallel", "arbitrary")),
        name="nsa_attn",
    )(main4, kcmp, vcmp, main4, main4, main4, main4, gates3, toep, cmpb, ovt, expand)


def _outproj_kernel(a1_ref, a2_ref, a3_ref, w_ref, h_ref, o_ref):
    r1 = a1_ref.shape[1]
    r2 = r1 + a2_ref.shape[1]
    o_ref[...] = h_ref[...] + ((_dot(a1_ref[...], w_ref[:r1, :]) + _dot(a2_ref[...], w_ref[r1:r2, :]))
                               + _dot(a3_ref[...], w_ref[r2:, :]))


def _outproj(a1, a2, a3, w, h, layer, tm=512):
    m = h.shape[0]
    act = lambda a: pl.BlockSpec((tm, a.shape[1]), lambda i: (i, 0))
    return pl.pallas_call(
        _outproj_kernel,
        grid=(m // tm,),
        in_specs=[act(a1), act(a2), act(a3),
                  pl.BlockSpec((None, D_MODEL, D_MODEL), lambda i: (layer, 0, 0), pipeline_mode=pl.Buffered(1)),
                  act(h)],
        out_specs=act(h),
        out_shape=jax.ShapeDtypeStruct(h.shape, _F32),
        compiler_params=_cparams(("parallel",)),
        name="outproj_residual",
    )(a1, a2, a3, w, h)


def _ffn_kernel(x_ref, halo_ref, g_ref, wg_ref, wu_ref, cwg_ref, cwu_ref, cbg_ref, cbu_ref, wd_ref, fg_ref, o_ref,
                xn_ref, hg_ref, hu_ref, *, tm, seq, final_norm):
    i = pl.program_id(0)
    j = pl.program_id(1)

    @pl.when(j == 0)
    def _():
        g = g_ref[...]
        x = x_ref[...]
        xn_ref[HALO:, :] = _rms(x, g).astype(_BF16)
        keep = jnp.where((i * tm) % seq == 0, 0.0, 1.0)
        xn_ref[:HALO, :] = (_rms(halo_ref[0], g) * keep).astype(_BF16)
        o_ref[...] = x

    xn = xn_ref[...]
    hg_ref[...] = _dot(xn, wg_ref[...])
    hu_ref[...] = _dot(xn, wu_ref[...])

    def conv(h_ref, cw_ref, cb_ref):
        out = cb_ref[...] + cw_ref[CONV_W - 1:CONV_W, :] * h_ref[HALO:, :]
        for tap in range(CONV_W - 1):
            shift = CONV_W - 1 - tap
            out = out + cw_ref[tap:tap + 1, :] * h_ref[HALO - shift:HALO - shift + tm, :]
        return out

    gate = conv(hg_ref, cwg_ref, cbg_ref)
    up = conv(hu_ref, cwu_ref, cbu_ref)
    act = gate / (1.0 + jnp.exp(-gate)) * up
    o_ref[...] += _dot(act.astype(_BF16), wd_ref[...])

    if final_norm:
        @pl.when(j == pl.num_programs(1) - 1)
        def _():
            o_ref[...] = _rms(o_ref[...], fg_ref[...])


def _ffn(h, g, w_up, conv_w, conv_b, w_down, final_g, layer, seq, final_norm, tm=1024, tf=512):
    m = h.shape[0]
    nf = D_FF // tf
    halo_view = h.reshape(m // HALO, HALO, D_MODEL)
    lay = lambda shape, imap: pl.BlockSpec((None,) + shape, lambda i, j: (layer,) + imap(i, j))
    return pl.pallas_call(
        functools.partial(_ffn_kernel, tm=tm, seq=seq, final_norm=final_norm),
        grid=(m // tm, nf),
        in_specs=[
            pl.BlockSpec((tm, D_MODEL), lambda i, j: (i, 0), pipeline_mode=pl.Buffered(1)),
            pl.BlockSpec((1, HALO, D_MODEL), lambda i, j: (jnp.maximum(i * (tm // HALO) - 1, 0), 0, 0)),
            lay((1, D_MODEL), lambda i, j: (0, 0)),
            lay((D_MODEL, tf), lambda i, j: (0, j)),
            lay((D_MODEL, tf), lambda i, j: (0, j + nf)),
            lay((CONV_W, tf), lambda i, j: (0, j)),
            lay((CONV_W, tf), lambda i, j: (0, j + nf)),
            lay((1, tf), lambda i, j: (0, j)),
            lay((1, tf), lambda i, j: (0, j + nf)),
            lay((tf, D_MODEL), lambda i, j: (j, 0)),
            pl.BlockSpec((1, D_MODEL), lambda i, j: (0, 0)),
        ],
        out_specs=pl.BlockSpec((tm, D_MODEL), lambda i, j: (i, 0)),
        out_shape=jax.ShapeDtypeStruct(h.shape, _F32),
        scratch_shapes=[
            pltpu.VMEM((tm + HALO, D_MODEL), _BF16),
            pltpu.VMEM((tm + HALO, tf), _F32),
            pltpu.VMEM((tm + HALO, tf), _F32),
        ],
        compiler_params=_cparams(("parallel", "arbitrary")),
        name="rms_conv_ffn",
    )(h, halo_view, g, w_up, w_up, conv_w, conv_w, conv_b, conv_b, w_down, final_g)


def _pad_lanes(v, width=LANE):
    return jnp.pad(v, ((0, 0), (0, width - v.shape[-1])))


def kernel(x, attn_norm_g, w_in, fox_f_bias, diff_lq1, diff_lk1, diff_lq2, diff_lk2, diff_subln_g, nsa_cmp_pos, nsa_cmp_wk1, nsa_cmp_wk2, nsa_cmp_wv1, nsa_cmp_wv2, w_out, ffn_norm_g, ffn_w_up, ffn_conv_w, ffn_conv_b, ffn_w_down, rel_bias, final_norm_g):
    b, s, _ = x.shape
    depth = w_in.shape[0]
    m = b * s
    t = ATT_T
    n_cmp = (s - CMP_LEN) // CMP_STRIDE + 1
    n_sel = s // SEL_LEN
    assert n_cmp + 1 == s // CMP_STRIDE == LANE and n_sel >= SEL_TOPN and s % t == 0

    w_main = jnp.concatenate([w_in[:, :, o:o + n] for o, n in _MAIN_SEGMENTS], axis=2).astype(_BF16)
    w_main = jnp.transpose(w_main.reshape(depth, D_MODEL, N_CHUNK, CHUNK_W), (0, 2, 1, 3))
    pad = lambda n: jnp.zeros((depth, D_MODEL, LANE - n), w_in.dtype)
    per_group = NSA_HPG * 3
    w_gate = jnp.concatenate(
        [w_in[:, :, _O_FF:_O_FF + FOX_HEADS], pad(FOX_HEADS)]
        + [a for g in range(NSA_KV_GROUPS)
           for a in (w_in[:, :, _O_NG + g * per_group:_O_NG + (g + 1) * per_group], pad(per_group))],
        axis=2).astype(_BF16)
    w_out_b = w_out.astype(_BF16)
    w_up_b = ffn_w_up.astype(_BF16)
    w_down_b = ffn_w_down.astype(_BF16)
    wk1, wk2 = nsa_cmp_wk1.astype(_BF16), nsa_cmp_wk2.astype(_BF16)
    wv1, wv2 = nsa_cmp_wv1.astype(_BF16), nsa_cmp_wv2.astype(_BF16)
    pos_flat = nsa_cmp_pos.reshape(depth, 1, CMP_LEN * HEAD_DIM).astype(_BF16)
    f_bias_rows = _pad_lanes(fox_f_bias)[:, None, :]
    lam_rows = jnp.stack([_pad_lanes(v) for v in (diff_lq1, diff_lk1, diff_lq2, diff_lk2)], axis=1)
    lam_rows = jnp.pad(lam_rows, ((0, 0), (0, 4), (0, 0)))
    attn_g, ffn_g, subln_g = attn_norm_g[:, None, :], ffn_norm_g[:, None, :], diff_subln_g[:, None, :]
    conv_b = ffn_conv_b[:, None, :]

    toep_idx = _toeplitz_bucket_tiles(t).reshape(3 * t, t)
    diff_bias = _bias_tables(rel_bias, toep_idx, DIFF_HEADS, 0).reshape(DIFF_HEADS, 3, t, t)
    nsa_bias = _bias_tables(rel_bias, toep_idx, NSA_HEADS, DIFF_HEADS).reshape(NSA_KV_GROUPS, NSA_HPG, 3, t, t)
    nsa_bias = jnp.transpose(nsa_bias, (0, 2, 1, 3, 4))
    cmp_bias = _bias_tables(rel_bias, _cmp_bucket_tile(s), NSA_HEADS, DIFF_HEADS)
    cmp_starts = np.arange(LANE) * CMP_STRIDE
    sel_starts = np.arange(n_sel) * SEL_LEN
    overlap = np.clip(np.minimum(cmp_starts[:, None] + CMP_LEN, sel_starts[None, :] + SEL_LEN)
                      - np.maximum(cmp_starts[:, None], sel_starts[None, :]), 0, None).astype(np.float32) / CMP_LEN
    overlap[n_cmp:] = 0.0
    ovt = jnp.asarray(overlap.T, _BF16)
    expand = np.zeros((s, LANE), np.float32)
    expand[np.arange(s), np.arange(s) // SEL_LEN] = 1.0
    expand = jnp.asarray(expand, _BF16)

    h = x.reshape(m, D_MODEL)
    for l in range(depth):
        main, gates = _inproj(h, attn_g, w_main, w_gate, l)
        main4 = main.reshape(N_CHUNK, b, s, CHUNK_W)
        gates3 = gates.reshape(b, s, N_GATE)

        qa, ka = _fox_prep(gates3, f_bias_rows, l)
        fox_o = _fox_attn(main4, qa, ka)

        lam_init = 0.8 - 0.6 * math.exp(-0.3 * l)
        diff_o = _diff_attn(main4, diff_bias, lam_rows, subln_g, l, lam_init)

        kcmp, vcmp = _nsa_compress(main4, pos_flat, wk1, wk2, wv1, wv2, l, n_cmp)
        nsa_o = _nsa_attn(main4, kcmp, vcmp, gates3, nsa_bias, cmp_bias, ovt, expand)

        h = _outproj(fox_o.reshape(m, -1), diff_o.reshape(m, -1), nsa_o.reshape(m, -1), w_out_b, h, l)
        h = _ffn(h, ffn_g, w_up_b, ffn_conv_w, conv_b, w_down_b, final_norm_g[None], l, s,
                 final_norm=(l == depth - 1))
    return h.reshape(b, s, D_MODEL)
```

```python
import functools
import math

import jax
import jax.numpy as jnp
import numpy as np
from jax import lax
from jax.experimental import pallas as pl
from jax.experimental.pallas import tpu as pltpu

D_MODEL = 2048
HEAD_DIM = 128
FOX_HEADS = 6
DIFF_HEADS = 4
DIFF_QK_DIM = HEAD_DIM // 2
NSA_HEADS = 6
NSA_KV_GROUPS = 2
NSA_HPG = NSA_HEADS // NSA_KV_GROUPS
CMP_LEN = 32
CMP_STRIDE = 16
SEL_LEN = 64
SEL_TOPN = 8
WINDOW = 512
N_BUCKETS = 32
MAX_DISTANCE = 128
D_FF = 5632
CONV_W = 3
EPS = 1e-6
NEG_INF = -1e30
FORCE_SCORE = 1e4

IN_SIZES = (
    FOX_HEADS * HEAD_DIM, FOX_HEADS * HEAD_DIM, FOX_HEADS * HEAD_DIM, FOX_HEADS,
    DIFF_HEADS * 2 * DIFF_QK_DIM, DIFF_HEADS * 2 * DIFF_QK_DIM, DIFF_HEADS * HEAD_DIM,
    NSA_HEADS * HEAD_DIM,
    NSA_KV_GROUPS * HEAD_DIM, NSA_KV_GROUPS * HEAD_DIM,
    NSA_KV_GROUPS * HEAD_DIM, NSA_KV_GROUPS * HEAD_DIM,
    NSA_KV_GROUPS * HEAD_DIM, NSA_KV_GROUPS * HEAD_DIM,
    NSA_HEADS * 3,
)
_IN_OFF = np.concatenate([[0], np.cumsum(IN_SIZES)])
(_O_FQ, _O_FK, _O_FV, _O_FF, _O_DQ, _O_DK, _O_DV, _O_NQ, _O_NKC, _O_NVC, _O_NKS, _O_NVS,
 _O_NKW, _O_NVW, _O_NG) = [int(v) for v in _IN_OFF[:-1]]

LANE = 128
HALO = 8
C_FQ, C_FK, C_FV, C_NQ = 0, 768, 1536, 2304
C_DQ, C_DK, C_DV = 3072, 3584, 4096
C_NKC, C_NVC, C_NKS, C_NVS, C_NKW, C_NVW = 4608, 4864, 5120, 5376, 5632, 5888
N_MAIN = 6144
CHUNK_W = 768
N_CHUNK = N_MAIN // CHUNK_W
N_GATE = 3 * LANE
ATT_T = 512
LOG2E = math.log2(math.e)
VMEM_LIMIT = 56 * 1024 * 1024

_F32 = jnp.float32
_BF16 = jnp.bfloat16


_MAIN_SEGMENTS = ((_O_FQ, 768), (_O_FK, 768), (_O_FV, 768), (_O_NQ, 768), (_O_DQ, 512), (_O_DK, 512),
                  (_O_DV, 512), (_O_NKC, 256), (_O_NVC, 256), (_O_NKS, 256), (_O_NVS, 256),
                  (_O_NKW, 256), (_O_NVW, 256))


def _t5_bucket_np(rel):
    n = np.maximum(rel, 0)
    max_exact = N_BUCKETS // 2
    nf = np.maximum(n, 1).astype(np.float32)
    large = max_exact + (np.log(nf / np.float32(max_exact)) / np.float32(math.log(MAX_DISTANCE / max_exact))
                         * np.float32(N_BUCKETS - max_exact)).astype(np.int32)
    large = np.minimum(large, N_BUCKETS - 1)
    return np.where(n < max_exact, n, large).astype(np.int32)


def _toeplitz_bucket_tiles(t):
    i = np.arange(t)[:, None]
    j = np.arange(t)[None, :]
    d0 = np.where(j <= i, _t5_bucket_np(i - j), -1)
    d1 = _t5_bucket_np(i - j + t)
    assert np.all(_t5_bucket_np(np.arange(t + 1, 8 * t)) == N_BUCKETS - 1)
    assert WINDOW in (t, 2 * t)
    edge = np.where(i < j, _t5_bucket_np(i - j + WINDOW), -1)
    return np.stack([d0, d1, edge]).astype(np.int32)


def _cmp_bucket_tile(s):
    n = np.arange(LANE)[None, :]
    t = np.arange(s)[:, None]
    n_cmp = (s - CMP_LEN) // CMP_STRIDE + 1
    block_end = n * CMP_STRIDE + CMP_LEN - 1
    ok = (block_end <= t) & (n < n_cmp)
    return np.where(ok, _t5_bucket_np(t - block_end), -1).astype(np.int32)


def _dot(a, b):
    return jnp.dot(a, b, preferred_element_type=_F32)


def _dot_nt(a, b):
    return lax.dot_general(a, b, (((1,), (1,)), ((), ())), preferred_element_type=_F32)


def _split3(x):
    hi = x.astype(_BF16)
    r = x - hi.astype(_F32)
    mid = r.astype(_BF16)
    lo = (r - mid.astype(_F32)).astype(_BF16)
    return hi, mid, lo


def _rms(x, g):
    return x * lax.rsqrt(jnp.mean(x * x, axis=-1, keepdims=True) + EPS) * g


def _cparams(sem, vmem=VMEM_LIMIT):
    return pltpu.CompilerParams(dimension_semantics=sem, vmem_limit_bytes=vmem)


def _cols(rows, col, width, where):
    per = CHUNK_W // width

    def index_map(*grid):
        batch, row_blk, off = where(*grid)
        blk = col // width + off
        return blk // per, batch, row_blk, blk % per

    return pl.BlockSpec((None, None, rows, width), index_map)


def _inproj_kernel(x_ref, g_ref, w_ref, wg_ref, cs_ref, main_ref, gate_ref, xn_ref):
    xn_ref[...] = _rms(x_ref[...], g_ref[...]).astype(_BF16)
    gate_ref[...] = _dot(xn_ref[...], wg_ref[...])
    for c in range(N_CHUNK):
        main_ref[c] = (_dot(xn_ref[...], w_ref[c]) * cs_ref[c]).astype(_BF16)


def _query_col_scale():
    cs = np.ones((1, N_MAIN), np.float32)
    cs[:, C_FQ:C_FQ + FOX_HEADS * HEAD_DIM] = HEAD_DIM ** -0.5 * LOG2E
    cs[:, C_NQ:C_NQ + NSA_HEADS * HEAD_DIM] = HEAD_DIM ** -0.5 * LOG2E
    cs[:, C_DQ:C_DQ + DIFF_HEADS * 2 * DIFF_QK_DIM] = DIFF_QK_DIM ** -0.5 * LOG2E
    return cs.reshape(N_CHUNK, 1, CHUNK_W)


def _inproj(h, g, w_main, w_gate, layer, tm=256):
    m = h.shape[0]
    once = pl.Buffered(1)
    return pl.pallas_call(
        _inproj_kernel,
        grid=(m // tm,),
        in_specs=[
            pl.BlockSpec((tm, D_MODEL), lambda i: (i, 0)),
            pl.BlockSpec((None, 1, D_MODEL), lambda i: (layer, 0, 0)),
            pl.BlockSpec((None, N_CHUNK, D_MODEL, CHUNK_W), lambda i: (layer, 0, 0, 0), pipeline_mode=once),
            pl.BlockSpec((None, D_MODEL, N_GATE), lambda i: (layer, 0, 0), pipeline_mode=once),
            pl.BlockSpec((N_CHUNK, 1, CHUNK_W), lambda i: (0, 0, 0), pipeline_mode=once),
        ],
        out_specs=[
            pl.BlockSpec((N_CHUNK, tm, CHUNK_W), lambda i: (0, i, 0)),
            pl.BlockSpec((tm, N_GATE), lambda i: (i, 0)),
        ],
        out_shape=[jax.ShapeDtypeStruct((N_CHUNK, m, CHUNK_W), _BF16), jax.ShapeDtypeStruct((m, N_GATE), _F32)],
        scratch_shapes=[pltpu.VMEM((tm, D_MODEL), _BF16)],
        compiler_params=_cparams(("parallel",)),
        name="rms_inproj",
    )(h, g, w_main, w_gate, jnp.asarray(_query_col_scale()))


def _bias_table_kernel(rel_ref, idx_ref, out_ref):
    h = pl.program_id(0)
    idx = idx_ref[...]
    acc = jnp.full(idx.shape, NEG_INF, _F32)
    last = rel_ref[N_BUCKETS - 1, h]
    for b in range(N_BUCKETS):
        acc = jnp.where(idx == b, (rel_ref[b, h] - last) * LOG2E, acc)
    out_ref[...] = acc


def _bias_tables(rel_bias, idx, n_heads, head0):
    r, c = idx.shape
    return pl.pallas_call(
        _bias_table_kernel,
        grid=(n_heads,),
        in_specs=[
            pl.BlockSpec(memory_space=pltpu.SMEM),
            pl.BlockSpec((r, c), lambda h: (0, 0)),
        ],
        out_specs=pl.BlockSpec((None, r, c), lambda h: (h, 0, 0)),
        out_shape=jax.ShapeDtypeStruct((n_heads, r, c), _F32),
        compiler_params=_cparams(("arbitrary",)),
        name="t5_bias_tables",
    )(rel_bias[:, head0:head0 + n_heads], jnp.asarray(idx))


def _fox_prep_kernel(gate_ref, fb_ref, qa_ref, ka_ref, *, chunk):
    s = gate_ref.shape[0]
    row = lax.broadcasted_iota(jnp.int32, (chunk, chunk), 0)
    col = lax.broadcasted_iota(jnp.int32, (chunk, chunk), 1)
    tri = jnp.where(col <= row, 1.0, 0.0).astype(_BF16)
    lane = lax.broadcasted_iota(jnp.int32, (chunk, LANE), 1)
    carry = jnp.zeros((1, LANE), _F32)
    for c in range(s // chunk):
        x = gate_ref[c * chunk:(c + 1) * chunk, :] + fb_ref[...]
        log_f = jnp.minimum(x, 0.0) - jnp.log1p(jnp.exp(-jnp.abs(x)))
        hi, mid, lo = _split3(log_f)
        cum = (_dot(tri, hi) + _dot(tri, mid)) + _dot(tri, lo) + carry
        carry = cum[chunk - 1:chunk, :]
        for h in range(FOX_HEADS):
            chi, cmid, clo = [p.astype(_F32) for p in _split3(cum[:, h:h + 1] * LOG2E)]
            qa = jnp.where(lane == 0, chi, jnp.where(lane == 1, cmid, jnp.where(lane == 2, clo,
                           jnp.where(lane < 6, 1.0, 0.0))))
            ka = jnp.where(lane < 3, 1.0, jnp.where(lane == 3, -chi, jnp.where(lane == 4, -cmid,
                           jnp.where(lane == 5, -clo, 0.0))))
            qa_ref[c * chunk:(c + 1) * chunk, h * LANE:(h + 1) * LANE] = qa.astype(_BF16)
            ka_ref[c * chunk:(c + 1) * chunk, h * LANE:(h + 1) * LANE] = ka.astype(_BF16)


def _fox_prep(gates3, f_bias_rows, layer):
    b, s, _ = gates3.shape
    w = FOX_HEADS * LANE
    return pl.pallas_call(
        functools.partial(_fox_prep_kernel, chunk=256),
        grid=(b,),
        in_specs=[
            pl.BlockSpec((None, s, LANE), lambda i: (i, 0, 0)),
            pl.BlockSpec((None, 1, LANE), lambda i: (layer, 0, 0)),
        ],
        out_specs=[
            pl.BlockSpec((None, s, w), lambda i: (i, 0, 0)),
            pl.BlockSpec((None, s, w), lambda i: (i, 0, 0)),
        ],
        out_shape=[jax.ShapeDtypeStruct((b, s, w), _BF16), jax.ShapeDtypeStruct((b, s, w), _BF16)],
        compiler_params=_cparams(("parallel",)),
        name="fox_prep",
    )(gates3, f_bias_rows)


def _online_update(s, v, m, l, acc):
    m_new = jnp.maximum(m, jnp.max(s, axis=-1, keepdims=True))
    alpha = jnp.exp2(m - m_new)
    p = jnp.exp2(s - m_new)
    l_new = alpha * l + jnp.sum(p, axis=-1, keepdims=True)
    acc_new = alpha * acc + _dot(p.astype(_BF16), v)
    return m_new, l_new, acc_new


def _softmax_init(rows):
    return (jnp.full((rows, 1), NEG_INF, _F32), jnp.zeros((rows, 1), _F32), jnp.zeros((rows, HEAD_DIM), _F32))


def _softmax_result(state):
    _, l, acc = state
    return acc / l


def _fox_attn_kernel(q_ref, k_ref, v_ref, qa_ref, ka_ref, o_ref, *, t):
    n_q = q_ref.shape[0] // t
    tile = lambda i: slice(i * t, (i + 1) * t)
    row = lax.broadcasted_iota(jnp.int32, (t, t), 0)
    col = lax.broadcasted_iota(jnp.int32, (t, t), 1)
    for qi in range(n_q):
        q = jnp.concatenate([q_ref[tile(qi), :], qa_ref[tile(qi), :]], axis=1)
        state = _softmax_init(t)
        for kj in range(qi + 1):
            s = _dot_nt(q, jnp.concatenate([k_ref[tile(kj), :], ka_ref[tile(kj), :]], axis=1))
            if kj == qi:
                s = jnp.where(col <= row, s, NEG_INF)
            state = _online_update(s, v_ref[tile(kj), :], *state)
        o_ref[tile(qi), :] = _softmax_result(state).astype(o_ref.dtype)


def _fox_attn(main4, qa, ka, t=ATT_T):
    _, b, s, _ = main4.shape
    return pl.pallas_call(
        functools.partial(_fox_attn_kernel, t=t),
        grid=(b, FOX_HEADS),
        in_specs=[
            _cols(s, C_FQ, LANE, lambda i, h: (i, 0, h)),
            _cols(s, C_FK, LANE, lambda i, h: (i, 0, h)),
            _cols(s, C_FV, LANE, lambda i, h: (i, 0, h)),
            pl.BlockSpec((None, s, LANE), lambda i, h: (i, 0, h)),
            pl.BlockSpec((None, s, LANE), lambda i, h: (i, 0, h)),
        ],
        out_specs=pl.BlockSpec((None, s, LANE), lambda i, h: (i, 0, h)),
        out_shape=jax.ShapeDtypeStruct((b, s, FOX_HEADS * HEAD_DIM), _BF16),
        compiler_params=_cparams(("parallel", "parallel")),
        name="fox_attn",
    )(main4, main4, main4, qa, ka)


def _diff_attn_kernel(q_ref, k_ref, v_ref, bias_ref, lam_ref, g_ref, o_ref, *, t, lam_init):
    n_q = q_ref.shape[0] // t
    tile = lambda i: slice(i * t, (i + 1) * t)
    lane = lax.broadcasted_iota(jnp.int32, (1, LANE), 1)
    first_map = jnp.where(lane < DIFF_QK_DIM, 1.0, 0.0).astype(q_ref.dtype)
    lv = lam_ref[...]
    lam = (jnp.exp(jnp.sum(lv[0:1, :] * lv[1:2, :], axis=-1, keepdims=True))
           - jnp.exp(jnp.sum(lv[2:3, :] * lv[3:4, :], axis=-1, keepdims=True)) + lam_init)
    for qi in range(n_q):
        q = q_ref[tile(qi), :]
        q1 = q * first_map
        q_both = jnp.concatenate([q1, q - q1], axis=0)
        state = _softmax_init(2 * t)
        for kj in range(qi + 1):
            s = _dot_nt(q_both, k_ref[tile(kj), :])
            if qi - kj < 2:
                s = (s.reshape(2, t, t) + bias_ref[qi - kj][None]).reshape(2 * t, t)
            state = _online_update(s, v_ref[tile(kj), :], *state)
        a = _softmax_result(state)
        o = a[:t] - lam * a[t:]
        o_ref[tile(qi), :] = (_rms(o, g_ref[...]) * (1.0 - lam_init)).astype(o_ref.dtype)


def _diff_attn(main4, bias, lam_rows, subln_g, layer, lam_init, t=ATT_T):
    _, b, s, _ = main4.shape
    return pl.pallas_call(
        functools.partial(_diff_attn_kernel, t=t, lam_init=lam_init),
        grid=(DIFF_HEADS, b),
        in_specs=[
            _cols(s, C_DQ, LANE, lambda h, i: (i, 0, h)),
            _cols(s, C_DK, LANE, lambda h, i: (i, 0, h)),
            _cols(s, C_DV, LANE, lambda h, i: (i, 0, h)),
            pl.BlockSpec((None, 3, t, t), lambda h, i: (h, 0, 0, 0)),
            pl.BlockSpec((None, 8, LANE), lambda h, i: (layer, 0, 0)),
            pl.BlockSpec((None, 1, LANE), lambda h, i: (layer, 0, 0)),
        ],
        out_specs=pl.BlockSpec((None, s, LANE), lambda h, i: (i, 0, h)),
        out_shape=jax.ShapeDtypeStruct((b, s, DIFF_HEADS * HEAD_DIM), _BF16),
        compiler_params=_cparams(("parallel", "parallel")),
        name="diff_attn",
    )(main4, main4, main4, bias, lam_rows, subln_g)


def _gelu_tanh(x):
    return 0.5 * x * (1.0 + jnp.tanh(math.sqrt(2.0 / math.pi) * (x + 0.044715 * (x * x * x))))


def _nsa_compress_kernel(k_ref, v_ref, pos_ref, wk1_ref, wk2_ref, wv1_ref, wv2_ref, kc_ref, vc_ref,
                         stage_ref, rows_ref, *, n_cmp):
    half = CMP_STRIDE * HEAD_DIM
    pos = pos_ref[...]
    rows = rows_ref.shape[0]
    keep = lax.broadcasted_iota(jnp.int32, (rows, HEAD_DIM), 0) < n_cmp

    def compress(x_ref, w1_ref, w2_ref, out_ref):
        stage_ref[...] = x_ref[...].astype(_F32)
        for l in range(CMP_STRIDE):
            rows_ref[:, l * HEAD_DIM:(l + 1) * HEAD_DIM] = (
                stage_ref[pl.ds(l, rows, stride=CMP_STRIDE), :].astype(_BF16))
        r = rows_ref[...]
        first = _dot(r, w1_ref[:half, :])
        second = _dot(r, w1_ref[half:, :])
        pre = first + pltpu.roll(second, rows - 1, 0) + _dot(pos, w1_ref[...])
        out = _dot(_gelu_tanh(pre).astype(_BF16), w2_ref[...])
        out_ref[...] = jnp.where(keep, out, 0.0).astype(out_ref.dtype)

    compress(k_ref, wk1_ref, wk2_ref, kc_ref)
    compress(v_ref, wv1_ref, wv2_ref, vc_ref)


def _nsa_compress(main4, pos_flat, wk1, wk2, wv1, wv2, layer, n_cmp):
    _, b, s, _ = main4.shape
    rows = s // CMP_STRIDE
    lay = lambda a: pl.BlockSpec((None,) + a.shape[1:], lambda i, j: (layer,) + (0,) * (a.ndim - 1))
    out_blk = pl.BlockSpec((None, None, rows, HEAD_DIM), lambda i, j: (i, j, 0, 0))
    out_sd = jax.ShapeDtypeStruct((b, NSA_KV_GROUPS, rows, HEAD_DIM), _BF16)
    return pl.pallas_call(
        functools.partial(_nsa_compress_kernel, n_cmp=n_cmp),
        grid=(b, NSA_KV_GROUPS),
        in_specs=[
            _cols(s, C_NKC, LANE, lambda i, j: (i, 0, j)),
            _cols(s, C_NVC, LANE, lambda i, j: (i, 0, j)),
            lay(pos_flat), lay(wk1), lay(wk2), lay(wv1), lay(wv2),
        ],
        out_specs=[out_blk, out_blk],
        out_shape=[out_sd, out_sd],
        scratch_shapes=[pltpu.VMEM((s, HEAD_DIM), _F32), pltpu.VMEM((rows, CMP_STRIDE * HEAD_DIM), _BF16)],
        compiler_params=_cparams(("parallel", "parallel")),
        name="nsa_compress",
    )(main4, main4, pos_flat, wk1, wk2, wv1, wv2)


def _nsa_attn_kernel(q_ref, kc_ref, vc_ref, ks_ref, vs_ref, kw_ref, vw_ref, gate_ref, toep_ref, cmpb_ref,
                     ovt_ref, exp_ref, o_ref, *, t):
    for qi in range(ks_ref.shape[0] // t):
        pl.when(pl.program_id(2) == qi)(functools.partial(
            _nsa_query_tile, qi, q_ref, kc_ref, vc_ref, ks_ref, vs_ref, kw_ref, vw_ref, gate_ref, toep_ref,
            cmpb_ref, ovt_ref, exp_ref, o_ref, t))


def _nsa_query_tile(qi, q_ref, kc_ref, vc_ref, ks_ref, vs_ref, kw_ref, vw_ref, gate_ref, toep_ref, cmpb_ref,
                    ovt_ref, exp_ref, o_ref, t):
    t0 = qi * t
    tile = lambda i: slice(i * t, (i + 1) * t)
    n_sel = ovt_ref.shape[0]
    qh = [q_ref[:, h * HEAD_DIM:(h + 1) * HEAD_DIM] for h in range(NSA_HPG)]

    tpos = t0 + lax.broadcasted_iota(jnp.int32, (t, 1), 0)
    row_ok = jnp.where(tpos >= CMP_LEN - 1, 1.0, 0.0)
    kc = kc_ref[...]
    vc = vc_ref[...]
    o_cmp = []
    psum = jnp.zeros((t, LANE), _F32)
    for h in range(NSA_HPG):
        sc = _dot_nt(qh[h], kc) + cmpb_ref[h]
        e = jnp.exp2(sc - jnp.max(sc, axis=-1, keepdims=True))
        p = e / jnp.sum(e, axis=-1, keepdims=True) * row_ok
        o_cmp.append(_dot(p.astype(_BF16), vc))
        psum = psum + p

    ovt = ovt_ref[...]
    p_hi, p_mid, p_lo = _split3(psum)
    imp = (_dot_nt(ovt, p_hi) + _dot_nt(ovt, p_mid)) + _dot_nt(ovt, p_lo)
    j = lax.broadcasted_iota(jnp.int32, (n_sel, t), 0)
    blk_t = (t0 + lax.broadcasted_iota(jnp.int32, (n_sel, t), 1)) // SEL_LEN
    valid = j <= blk_t
    forced = (j == 0) | (j == blk_t) | (j == blk_t - 1)
    score = jnp.where(valid, imp + jnp.where(forced, FORCE_SCORE, 0.0), NEG_INF)
    rank = jnp.zeros((n_sel, t), _F32)
    for k in range(n_sel):
        sk = score[k:k + 1, :]
        ahead = (sk > score) | ((sk == score) & (j > k))
        rank = rank + jnp.where(ahead, 1.0, 0.0)
    drop_t = jnp.where(valid & (rank < SEL_TOPN), 0.0, NEG_INF)
    drop_t = jnp.concatenate([drop_t, jnp.zeros((LANE - n_sel, t), _F32)], axis=0)
    drop = jnp.transpose(drop_t).astype(_BF16)

    q_all = jnp.concatenate(qh, axis=0)
    rows = NSA_HPG * t

    def add_bias(s, which):
        return (s.reshape(NSA_HPG, t, t) + toep_ref[which]).reshape(rows, t)

    q_sel = jnp.concatenate([q_all, jnp.concatenate([drop] * NSA_HPG, axis=0)], axis=1)

    state = _softmax_init(rows)
    for kj in range(qi + 1):
        s = _dot_nt(q_sel, jnp.concatenate([ks_ref[tile(kj), :], exp_ref[tile(kj), :]], axis=1))
        if qi - kj < 2:
            s = add_bias(s, qi - kj)
        state = _online_update(s, vs_ref[tile(kj), :], *state)
    o_sel = _softmax_result(state)

    d_edge = WINDOW // t
    state = _softmax_init(rows)
    for kj in range(max(qi - d_edge, 0), qi + 1):
        d = qi - kj
        s = add_bias(_dot_nt(q_all, kw_ref[tile(kj), :]), 2 if d == d_edge else d)
        state = _online_update(s, vw_ref[tile(kj), :], *state)
    o_win = _softmax_result(state)

    gl = gate_ref[...]
    gates = 1.0 / (1.0 + jnp.exp(-gl))
    for h in range(NSA_HPG):
        hs = slice(h * t, (h + 1) * t)
        out = (gates[:, 3 * h:3 * h + 1] * o_cmp[h]
               + gates[:, 3 * h + 1:3 * h + 2] * o_sel[hs]
               + gates[:, 3 * h + 2:3 * h + 3] * o_win[hs])
        o_ref[:, h * HEAD_DIM:(h + 1) * HEAD_DIM] = out.astype(o_ref.dtype)


def _nsa_attn(main4, kcmp, vcmp, gates3, toep, cmpb, ovt, expand, t=ATT_T):
    _, b, s, _ = main4.shape
    gw = NSA_HPG * HEAD_DIM
    kv = lambda c: _cols(s, c, LANE, lambda g, i, q: (i, 0, g))
    cmp_blk = pl.BlockSpec((None, None, LANE, HEAD_DIM), lambda g, i, q: (i, g, 0, 0))
    return pl.pallas_call(
        functools.partial(_nsa_attn_kernel, t=t),
        grid=(NSA_KV_GROUPS, b, s // t),
        in_specs=[
            _cols(t, C_NQ, gw, lambda g, i, q: (i, q, g)),
            cmp_blk, cmp_blk,
            kv(C_NKS), kv(C_NVS), kv(C_NKW), kv(C_NVW),
            pl.BlockSpec((None, t, LANE), lambda g, i, q: (i, q, 1 + g)),
            pl.BlockSpec((None, 3, NSA_HPG, t, t), lambda g, i, q: (g, 0, 0, 0, 0)),
            pl.BlockSpec((NSA_HPG, t, LANE), lambda g, i, q: (g, q, 0)),
            pl.BlockSpec(ovt.shape, lambda g, i, q: (0, 0)),
            pl.BlockSpec(expand.shape, lambda g, i, q: (0, 0)),
        ],
        out_specs=pl.BlockSpec((None, t, gw), lambda g, i, q: (i, q, g)),
        out_shape=jax.ShapeDtypeStruct((b, s, NSA_HEADS * HEAD_DIM), _BF16),
        compiler_params=_cparams(("parallel", "parallel", "arbitrary")),
        name="nsa_attn",
    )(main4, kcmp, vcmp, main4, main4, main4, main4, gates3, toep, cmpb, ovt, expand)


def _outproj_kernel(a1_ref, a2_ref, a3_ref, w_ref, h_ref, o_ref):
    r1 = a1_ref.shape[1]
    r2 = r1 + a2_ref.shape[1]
    o_ref[...] = h_ref[...] + ((_dot(a1_ref[...], w_ref[:r1, :]) + _dot(a2_ref[...], w_ref[r1:r2, :]))
                               + _dot(a3_ref[...], w_ref[r2:, :]))


def _outproj(a1, a2, a3, w, h, layer, tm=512):
    m = h.shape[0]
    act = lambda a: pl.BlockSpec((tm, a.shape[1]), lambda i: (i, 0))
    return pl.pallas_call(
        _outproj_kernel,
        grid=(m // tm,),
        in_specs=[act(a1), act(a2), act(a3),
                  pl.BlockSpec((None, D_MODEL, D_MODEL), lambda i: (layer, 0, 0), pipeline_mode=pl.Buffered(1)),
                  act(h)],
        out_specs=act(h),
        out_shape=jax.ShapeDtypeStruct(h.shape, _F32),
        compiler_params=_cparams(("parallel",)),
        name="outproj_residual",
    )(a1, a2, a3, w, h)


def _ffn_kernel(x_ref, halo_ref, g_ref, wg_ref, wu_ref, cwg_ref, cwu_ref, cbg_ref, cbu_ref, wd_ref, fg_ref, o_ref,
                xn_ref, hg_ref, hu_ref, *, tm, seq, final_norm):
    i = pl.program_id(0)
    j = pl.program_id(1)

    @pl.when(j == 0)
    def _():
        g = g_ref[...]
        x = x_ref[...]
        xn_ref[HALO:, :] = _rms(x, g).astype(_BF16)
        keep = jnp.where((i * tm) % seq == 0, 0.0, 1.0)
        xn_ref[:HALO, :] = (_rms(halo_ref[0], g) * keep).astype(_BF16)
        o_ref[...] = x

    xn = xn_ref[...]
    hg_ref[...] = _dot(xn, wg_ref[...])
    hu_ref[...] = _dot(xn, wu_ref[...])

    def conv(h_ref, cw_ref, cb_ref):
        out = cb_ref[...] + cw_ref[CONV_W - 1:CONV_W, :] * h_ref[HALO:, :]
        for tap in range(CONV_W - 1):
            shift = CONV_W - 1 - tap
            out = out + cw_ref[tap:tap + 1, :] * h_ref[HALO - shift:HALO - shift + tm, :]
        return out

    gate = conv(hg_ref, cwg_ref, cbg_ref)
    up = conv(hu_ref, cwu_ref, cbu_ref)
    act = gate / (1.0 + jnp.exp(-gate)) * up
    o_ref[...] += _dot(act.astype(_BF16), wd_ref[...])

    if final_norm:
        @pl.when(j == pl.num_programs(1) - 1)
        def _():
            o_ref[...] = _rms(o_ref[...], fg_ref[...])


def _ffn(h, g, w_up, conv_w, conv_b, w_down, final_g, layer, seq, final_norm, tm=512, tf=512):
    m = h.shape[0]
    nf = D_FF // tf
    halo_view = h.reshape(m // HALO, HALO, D_MODEL)
    lay = lambda shape, imap: pl.BlockSpec((None,) + shape, lambda i, j: (layer,) + imap(i, j))
    return pl.pallas_call(
        functools.partial(_ffn_kernel, tm=tm, seq=seq, final_norm=final_norm),
        grid=(m // tm, nf),
        in_specs=[
            pl.BlockSpec((tm, D_MODEL), lambda i, j: (i, 0)),
            pl.BlockSpec((1, HALO, D_MODEL), lambda i, j: (jnp.maximum(i * (tm // HALO) - 1, 0), 0, 0)),
            lay((1, D_MODEL), lambda i, j: (0, 0)),
            lay((D_MODEL, tf), lambda i, j: (0, j)),
            lay((D_MODEL, tf), lambda i, j: (0, j + nf)),
            lay((CONV_W, tf), lambda i, j: (0, j)),
            lay((CONV_W, tf), lambda i, j: (0, j + nf)),
            lay((1, tf), lambda i, j: (0, j)),
            lay((1, tf), lambda i, j: (0, j + nf)),
            lay((tf, D_MODEL), lambda i, j: (j, 0)),
            pl.BlockSpec((1, D_MODEL), lambda i, j: (0, 0)),
        ],
        out_specs=pl.BlockSpec((tm, D_MODEL), lambda i, j: (i, 0)),
        out_shape=jax.ShapeDtypeStruct(h.shape, _F32),
        scratch_shapes=[
            pltpu.VMEM((tm + HALO, D_MODEL), _BF16),
            pltpu.VMEM((tm + HALO, tf), _F32),
            pltpu.VMEM((tm + HALO, tf), _F32),
        ],
        compiler_params=_cparams(("parallel", "arbitrary")),
        name="rms_conv_ffn",
    )(h, halo_view, g, w_up, w_up, conv_w, conv_w, conv_b, conv_b, w_down, final_g)


def _pad_lanes(v, width=LANE):
    return jnp.pad(v, ((0, 0), (0, width - v.shape[-1])))


def kernel(x, attn_norm_g, w_in, fox_f_bias, diff_lq1, diff_lk1, diff_lq2, diff_lk2, diff_subln_g, nsa_cmp_pos, nsa_cmp_wk1, nsa_cmp_wk2, nsa_cmp_wv1, nsa_cmp_wv2, w_out, ffn_norm_g, ffn_w_up, ffn_conv_w, ffn_conv_b, ffn_w_down, rel_bias, final_norm_g):
    b, s, _ = x.shape
    depth = w_in.shape[0]
    m = b * s
    t = ATT_T
    n_cmp = (s - CMP_LEN) // CMP_STRIDE + 1
    n_sel = s // SEL_LEN
    assert n_cmp + 1 == s // CMP_STRIDE == LANE and n_sel >= SEL_TOPN and s % t == 0

    w_main = jnp.concatenate([w_in[:, :, o:o + n] for o, n in _MAIN_SEGMENTS], axis=2).astype(_BF16)
    w_main = jnp.transpose(w_main.reshape(depth, D_MODEL, N_CHUNK, CHUNK_W), (0, 2, 1, 3))
    pad = lambda n: jnp.zeros((depth, D_MODEL, LANE - n), w_in.dtype)
    per_group = NSA_HPG * 3
    w_gate = jnp.concatenate(
        [w_in[:, :, _O_FF:_O_FF + FOX_HEADS], pad(FOX_HEADS)]
        + [a for g in range(NSA_KV_GROUPS)
           for a in (w_in[:, :, _O_NG + g * per_group:_O_NG + (g + 1) * per_group], pad(per_group))],
        axis=2).astype(_BF16)
    w_out_b = w_out.astype(_BF16)
    w_up_b = ffn_w_up.astype(_BF16)
    w_down_b = ffn_w_down.astype(_BF16)
    wk1, wk2 = nsa_cmp_wk1.astype(_BF16), nsa_cmp_wk2.astype(_BF16)
    wv1, wv2 = nsa_cmp_wv1.astype(_BF16), nsa_cmp_wv2.astype(_BF16)
    pos_flat = nsa_cmp_pos.reshape(depth, 1, CMP_LEN * HEAD_DIM).astype(_BF16)
    f_bias_rows = _pad_lanes(fox_f_bias)[:, None, :]
    lam_rows = jnp.stack([_pad_lanes(v) for v in (diff_lq1, diff_lk1, diff_lq2, diff_lk2)], axis=1)
    lam_rows = jnp.pad(lam_rows, ((0, 0), (0, 4), (0, 0)))
    attn_g, ffn_g, subln_g = attn_norm_g[:, None, :], ffn_norm_g[:, None, :], diff_subln_g[:, None, :]
    conv_b = ffn_conv_b[:, None, :]

    toep_idx = _toeplitz_bucket_tiles(t).reshape(3 * t, t)
    diff_bias = _bias_tables(rel_bias, toep_idx, DIFF_HEADS, 0).reshape(DIFF_HEADS, 3, t, t)
    nsa_bias = _bias_tables(rel_bias, toep_idx, NSA_HEADS, DIFF_HEADS).reshape(NSA_KV_GROUPS, NSA_HPG, 3, t, t)
    nsa_bias = jnp.transpose(nsa_bias, (0, 2, 1, 3, 4))
    cmp_bias = _bias_tables(rel_bias, _cmp_bucket_tile(s), NSA_HEADS, DIFF_HEADS)
    cmp_starts = np.arange(LANE) * CMP_STRIDE
    sel_starts = np.arange(n_sel) * SEL_LEN
    overlap = np.clip(np.minimum(cmp_starts[:, None] + CMP_LEN, sel_starts[None, :] + SEL_LEN)
                      - np.maximum(cmp_starts[:, None], sel_starts[None, :]), 0, None).astype(np.float32) / CMP_LEN
    overlap[n_cmp:] = 0.0
    ovt = jnp.asarray(overlap.T, _BF16)
    expand = np.zeros((s, LANE), np.float32)
    expand[np.arange(s), np.arange(s) // SEL_LEN] = 1.0
    expand = jnp.asarray(expand, _BF16)

    h = x.reshape(m, D_MODEL)
    for l in range(depth):
        main, gates = _inproj(h, attn_g, w_main, w_gate, l)
        main4 = main.reshape(N_CHUNK, b, s, CHUNK_W)
        gates3 = gates.reshape(b, s, N_GATE)

        qa, ka = _fox_prep(gates3, f_bias_rows, l)
        fox_o = _fox_attn(main4, qa, ka)

        lam_init = 0.8 - 0.6 * math.exp(-0.3 * l)
        diff_o = _diff_attn(main4, diff_bias, lam_rows, subln_g, l, lam_init)

        kcmp, vcmp = _nsa_compress(main4, pos_flat, wk1, wk2, wv1, wv2, l, n_cmp)
        nsa_o = _nsa_attn(main4, kcmp, vcmp, gates3, nsa_bias, cmp_bias, ovt, expand)

        h = _outproj(fox_o.reshape(m, -1), diff_o.reshape(m, -1), nsa_o.reshape(m, -1), w_out_b, h, l)
        h = _ffn(h, ffn_g, w_up_b, ffn_conv_w, conv_b, w_down_b, final_norm_g[None], l, s,
                 final_norm=(l == depth - 1))
    return h.reshape(b, s, D_MODEL)
```

```python
import functools
import math

import jax
import jax.numpy as jnp
import numpy as np
from jax import lax
from jax.experimental import pallas as pl
from jax.experimental.pallas import tpu as pltpu

D_MODEL = 2048
HEAD_DIM = 128
FOX_HEADS = 6
DIFF_HEADS = 4
DIFF_QK_DIM = HEAD_DIM // 2
NSA_HEADS = 6
NSA_KV_GROUPS = 2
NSA_HPG = NSA_HEADS // NSA_KV_GROUPS
CMP_LEN = 32
CMP_STRIDE = 16
SEL_LEN = 64
SEL_TOPN = 8
WINDOW = 512
N_BUCKETS = 32
MAX_DISTANCE = 128
D_FF = 5632
CONV_W = 3
EPS = 1e-6
NEG_INF = -1e30
FORCE_SCORE = 1e4

IN_SIZES = (
    FOX_HEADS * HEAD_DIM, FOX_HEADS * HEAD_DIM, FOX_HEADS * HEAD_DIM, FOX_HEADS,
    DIFF_HEADS * 2 * DIFF_QK_DIM, DIFF_HEADS * 2 * DIFF_QK_DIM, DIFF_HEADS * HEAD_DIM,
    NSA_HEADS * HEAD_DIM,
    NSA_KV_GROUPS * HEAD_DIM, NSA_KV_GROUPS * HEAD_DIM,
    NSA_KV_GROUPS * HEAD_DIM, NSA_KV_GROUPS * HEAD_DIM,
    NSA_KV_GROUPS * HEAD_DIM, NSA_KV_GROUPS * HEAD_DIM,
    NSA_HEADS * 3,
)
_IN_OFF = np.concatenate([[0], np.cumsum(IN_SIZES)])
(_O_FQ, _O_FK, _O_FV, _O_FF, _O_DQ, _O_DK, _O_DV, _O_NQ, _O_NKC, _O_NVC, _O_NKS, _O_NVS,
 _O_NKW, _O_NVW, _O_NG) = [int(v) for v in _IN_OFF[:-1]]

LANE = 128
HALO = 8
C_FQ, C_FK, C_FV, C_NQ = 0, 768, 1536, 2304
C_DQ, C_DK, C_DV = 3072, 3584, 4096
C_NKC, C_NVC, C_NKS, C_NVS, C_NKW, C_NVW = 4608, 4864, 5120, 5376, 5632, 5888
N_MAIN = 6144
CHUNK_W = 768
N_CHUNK = N_MAIN // CHUNK_W
N_GATE = 3 * LANE
ATT_T = 256
NSA_T = 512
LOG2E = math.log2(math.e)
VMEM_LIMIT = 56 * 1024 * 1024

_F32 = jnp.float32
_BF16 = jnp.bfloat16


_MAIN_SEGMENTS = ((_O_FQ, 768), (_O_FK, 768), (_O_FV, 768), (_O_NQ, 768), (_O_DQ, 512), (_O_DK, 512),
                  (_O_DV, 512), (_O_NKC, 256), (_O_NVC, 256), (_O_NKS, 256), (_O_NVS, 256),
                  (_O_NKW, 256), (_O_NVW, 256))


def _t5_bucket_np(rel):
    n = np.maximum(rel, 0)
    max_exact = N_BUCKETS // 2
    nf = np.maximum(n, 1).astype(np.float32)
    large = max_exact + (np.log(nf / np.float32(max_exact)) / np.float32(math.log(MAX_DISTANCE / max_exact))
                         * np.float32(N_BUCKETS - max_exact)).astype(np.int32)
    large = np.minimum(large, N_BUCKETS - 1)
    return np.where(n < max_exact, n, large).astype(np.int32)


def _toeplitz_bucket_tiles(t):
    i = np.arange(t)[:, None]
    j = np.arange(t)[None, :]
    d0 = np.where(j <= i, _t5_bucket_np(i - j), -1)
    d1 = _t5_bucket_np(i - j + t)
    assert np.all(_t5_bucket_np(np.arange(t + 1, 8 * t)) == N_BUCKETS - 1)
    assert WINDOW in (t, 2 * t)
    edge = np.where(i < j, _t5_bucket_np(i - j + WINDOW), -1)
    return np.stack([d0, d1, edge]).astype(np.int32)


def _cmp_bucket_tile(s):
    n = np.arange(LANE)[None, :]
    t = np.arange(s)[:, None]
    n_cmp = (s - CMP_LEN) // CMP_STRIDE + 1
    block_end = n * CMP_STRIDE + CMP_LEN - 1
    ok = (block_end <= t) & (n < n_cmp)
    return np.where(ok, _t5_bucket_np(t - block_end), -1).astype(np.int32)


def _dot(a, b):
    return jnp.dot(a, b, preferred_element_type=_F32)


def _dot_nt(a, b):
    return lax.dot_general(a, b, (((1,), (1,)), ((), ())), preferred_element_type=_F32)


def _split3(x):
    hi = x.astype(_BF16)
    r = x - hi.astype(_F32)
    mid = r.astype(_BF16)
    lo = (r - mid.astype(_F32)).astype(_BF16)
    return hi, mid, lo


def _rms(x, g):
    return x * lax.rsqrt(jnp.mean(x * x, axis=-1, keepdims=True) + EPS) * g


def _cparams(sem, vmem=VMEM_LIMIT):
    return pltpu.CompilerParams(dimension_semantics=sem, vmem_limit_bytes=vmem)


def _cols(rows, col, width, where):
    per = CHUNK_W // width

    def index_map(*grid):
        batch, row_blk, off = where(*grid)
        blk = col // width + off
        return blk // per, batch, row_blk, blk % per

    return pl.BlockSpec((None, None, rows, width), index_map)


def _inproj_kernel(x_ref, g_ref, w_ref, wg_ref, cs_ref, main_ref, gate_ref, xn_ref):
    xn_ref[...] = _rms(x_ref[...], g_ref[...]).astype(_BF16)
    gate_ref[...] = _dot(xn_ref[...], wg_ref[...])
    for c in range(N_CHUNK):
        w = w_ref[:, c * CHUNK_W:(c + 1) * CHUNK_W]
        main_ref[c] = (_dot(xn_ref[...], w) * cs_ref[c]).astype(_BF16)


def _query_col_scale():
    cs = np.ones((1, N_MAIN), np.float32)
    cs[:, C_FQ:C_FQ + FOX_HEADS * HEAD_DIM] = HEAD_DIM ** -0.5 * LOG2E
    cs[:, C_NQ:C_NQ + NSA_HEADS * HEAD_DIM] = HEAD_DIM ** -0.5 * LOG2E
    cs[:, C_DQ:C_DQ + DIFF_HEADS * 2 * DIFF_QK_DIM] = DIFF_QK_DIM ** -0.5 * LOG2E
    return cs.reshape(N_CHUNK, 1, CHUNK_W)


def _inproj(h, g, w_main, w_gate, layer, tm=256):
    m = h.shape[0]
    once = pl.Buffered(1)
    return pl.pallas_call(
        _inproj_kernel,
        grid=(m // tm,),
        in_specs=[
            pl.BlockSpec((tm, D_MODEL), lambda i: (i, 0)),
            pl.BlockSpec((None, 1, D_MODEL), lambda i: (layer, 0, 0)),
            pl.BlockSpec((None, D_MODEL, N_MAIN), lambda i: (layer, 0, 0), pipeline_mode=once),
            pl.BlockSpec((None, D_MODEL, N_GATE), lambda i: (layer, 0, 0), pipeline_mode=once),
            pl.BlockSpec((N_CHUNK, 1, CHUNK_W), lambda i: (0, 0, 0), pipeline_mode=once),
        ],
        out_specs=[
            pl.BlockSpec((N_CHUNK, tm, CHUNK_W), lambda i: (0, i, 0)),
            pl.BlockSpec((tm, N_GATE), lambda i: (i, 0)),
        ],
        out_shape=[jax.ShapeDtypeStruct((N_CHUNK, m, CHUNK_W), _BF16), jax.ShapeDtypeStruct((m, N_GATE), _F32)],
        scratch_shapes=[pltpu.VMEM((tm, D_MODEL), _BF16)],
        compiler_params=_cparams(("parallel",)),
        name="rms_inproj",
    )(h, g, w_main, w_gate, jnp.asarray(_query_col_scale()))


def _bias_table_kernel(rel_ref, idx_ref, out_ref):
    h = pl.program_id(0)
    idx = idx_ref[...]
    acc = jnp.full(idx.shape, NEG_INF, _F32)
    last = rel_ref[N_BUCKETS - 1, h]
    for b in range(N_BUCKETS):
        acc = jnp.where(idx == b, (rel_ref[b, h] - last) * LOG2E, acc)
    out_ref[...] = acc


def _bias_tables(rel_bias, idx, n_heads, head0):
    r, c = idx.shape
    return pl.pallas_call(
        _bias_table_kernel,
        grid=(n_heads,),
        in_specs=[
            pl.BlockSpec(memory_space=pltpu.SMEM),
            pl.BlockSpec((r, c), lambda h: (0, 0)),
        ],
        out_specs=pl.BlockSpec((None, r, c), lambda h: (h, 0, 0)),
        out_shape=jax.ShapeDtypeStruct((n_heads, r, c), _F32),
        compiler_params=_cparams(("arbitrary",)),
        name="t5_bias_tables",
    )(rel_bias[:, head0:head0 + n_heads], jnp.asarray(idx))


def _fox_prep_kernel(gate_ref, fb_ref, eq_ref, ek_ref, cq_ref, ck_ref, qa_ref, ka_ref, *, chunk):
    s = gate_ref.shape[0]
    row = lax.broadcasted_iota(jnp.int32, (chunk, chunk), 0)
    col = lax.broadcasted_iota(jnp.int32, (chunk, chunk), 1)
    tri = jnp.where(col <= row, 1.0, 0.0).astype(_BF16)
    carry = jnp.zeros((1, LANE), _F32)
    for c in range(s // chunk):
        rows = slice(c * chunk, (c + 1) * chunk)
        x = gate_ref[rows, :] + fb_ref[...]
        log_f = jnp.minimum(x, 0.0) - jnp.log1p(jnp.exp(-jnp.abs(x)))
        hi, mid, lo = _split3(log_f)
        cum = (_dot(tri, hi) + _dot(tri, mid)) + _dot(tri, lo) + carry
        carry = cum[chunk - 1:chunk, :]
        parts = jnp.concatenate(_split3(cum * LOG2E), axis=1)
        qa_ref[rows, :] = (_dot(parts, eq_ref[...]) + cq_ref[...]).astype(_BF16)
        ka_ref[rows, :] = (_dot(parts, ek_ref[...]) + ck_ref[...]).astype(_BF16)


def _fox_scatter_constants():
    w = FOX_HEADS * LANE
    eq, ek = np.zeros((3 * LANE, w), np.float32), np.zeros((3 * LANE, w), np.float32)
    cq, ck = np.zeros((1, w), np.float32), np.zeros((1, w), np.float32)
    for h in range(FOX_HEADS):
        for part in range(3):
            eq[part * LANE + h, h * LANE + part] = 1.0
            ek[part * LANE + h, h * LANE + 3 + part] = -1.0
        cq[0, h * LANE + 3:h * LANE + 6] = 1.0
        ck[0, h * LANE:h * LANE + 3] = 1.0
    return jnp.asarray(eq, _BF16), jnp.asarray(ek, _BF16), jnp.asarray(cq), jnp.asarray(ck)


def _fox_prep(gates3, f_bias_rows, layer):
    b, s, _ = gates3.shape
    w = FOX_HEADS * LANE
    consts = _fox_scatter_constants()
    return pl.pallas_call(
        functools.partial(_fox_prep_kernel, chunk=256),
        grid=(b,),
        in_specs=[
            pl.BlockSpec((None, s, LANE), lambda i: (i, 0, 0)),
            pl.BlockSpec((None, 1, LANE), lambda i: (layer, 0, 0)),
        ] + [pl.BlockSpec(c.shape, lambda i: (0, 0)) for c in consts],
        out_specs=[
            pl.BlockSpec((None, s, w), lambda i: (i, 0, 0)),
            pl.BlockSpec((None, s, w), lambda i: (i, 0, 0)),
        ],
        out_shape=[jax.ShapeDtypeStruct((b, s, w), _BF16), jax.ShapeDtypeStruct((b, s, w), _BF16)],
        compiler_params=_cparams(("parallel",)),
        name="fox_prep",
    )(gates3, f_bias_rows, *consts)


def _online_update(s, v, m, l, acc):
    m_new = jnp.maximum(m, jnp.max(s, axis=-1, keepdims=True))
    alpha = jnp.exp2(m - m_new)
    p = jnp.exp2(s - m_new)
    l_new = alpha * l + jnp.sum(p, axis=-1, keepdims=True)
    acc_new = alpha * acc + _dot(p.astype(_BF16), v)
    return m_new, l_new, acc_new


def _softmax_init(rows):
    return (jnp.full((rows, 1), NEG_INF, _F32), jnp.zeros((rows, 1), _F32), jnp.zeros((rows, HEAD_DIM), _F32))


def _softmax_result(state):
    _, l, acc = state
    return acc / l


def _fox_attn_kernel(q_ref, k_ref, v_ref, qa_ref, ka_ref, o_ref, *, t):
    n_q = q_ref.shape[0] // t
    tile = lambda i: slice(i * t, (i + 1) * t)
    row = lax.broadcasted_iota(jnp.int32, (t, t), 0)
    col = lax.broadcasted_iota(jnp.int32, (t, t), 1)
    for qi in range(n_q):
        q = jnp.concatenate([q_ref[tile(qi), :], qa_ref[tile(qi), :]], axis=1)
        state = _softmax_init(t)
        for kj in range(qi + 1):
            s = _dot_nt(q, jnp.concatenate([k_ref[tile(kj), :], ka_ref[tile(kj), :]], axis=1))
            if kj == qi:
                s = jnp.where(col <= row, s, NEG_INF)
            state = _online_update(s, v_ref[tile(kj), :], *state)
        o_ref[tile(qi), :] = _softmax_result(state).astype(o_ref.dtype)


def _fox_attn(main4, qa, ka, t=ATT_T):
    _, b, s, _ = main4.shape
    return pl.pallas_call(
        functools.partial(_fox_attn_kernel, t=t),
        grid=(b, FOX_HEADS),
        in_specs=[
            _cols(s, C_FQ, LANE, lambda i, h: (i, 0, h)),
            _cols(s, C_FK, LANE, lambda i, h: (i, 0, h)),
            _cols(s, C_FV, LANE, lambda i, h: (i, 0, h)),
            pl.BlockSpec((None, s, LANE), lambda i, h: (i, 0, h)),
            pl.BlockSpec((None, s, LANE), lambda i, h: (i, 0, h)),
        ],
        out_specs=pl.BlockSpec((None, s, LANE), lambda i, h: (i, 0, h)),
        out_shape=jax.ShapeDtypeStruct((b, s, FOX_HEADS * HEAD_DIM), _BF16),
        compiler_params=_cparams(("parallel", "parallel")),
        name="fox_attn",
    )(main4, main4, main4, qa, ka)


def _diff_attn_kernel(q_ref, k_ref, v_ref, bias_ref, lam_ref, g_ref, o_ref, *, t, lam_init):
    n_q = q_ref.shape[0] // t
    tile = lambda i: slice(i * t, (i + 1) * t)
    lane = lax.broadcasted_iota(jnp.int32, (1, LANE), 1)
    first_map = jnp.where(lane < DIFF_QK_DIM, 1.0, 0.0).astype(q_ref.dtype)
    lv = lam_ref[...]
    lam = (jnp.exp(jnp.sum(lv[0:1, :] * lv[1:2, :], axis=-1, keepdims=True))
           - jnp.exp(jnp.sum(lv[2:3, :] * lv[3:4, :], axis=-1, keepdims=True)) + lam_init)
    for qi in range(n_q):
        q = q_ref[tile(qi), :]
        q1 = q * first_map
        q_both = jnp.concatenate([q1, q - q1], axis=0)
        state = _softmax_init(2 * t)
        for kj in range(qi + 1):
            s = _dot_nt(q_both, k_ref[tile(kj), :])
            if qi - kj < 2:
                s = (s.reshape(2, t, t) + bias_ref[qi - kj][None]).reshape(2 * t, t)
            state = _online_update(s, v_ref[tile(kj), :], *state)
        a = _softmax_result(state)
        o = a[:t] - lam * a[t:]
        o_ref[tile(qi), :] = (_rms(o, g_ref[...]) * (1.0 - lam_init)).astype(o_ref.dtype)


def _diff_attn(main4, bias, lam_rows, subln_g, layer, lam_init, t=ATT_T):
    _, b, s, _ = main4.shape
    return pl.pallas_call(
        functools.partial(_diff_attn_kernel, t=t, lam_init=lam_init),
        grid=(DIFF_HEADS, b),
        in_specs=[
            _cols(s, C_DQ, LANE, lambda h, i: (i, 0, h)),
            _cols(s, C_DK, LANE, lambda h, i: (i, 0, h)),
            _cols(s, C_DV, LANE, lambda h, i: (i, 0, h)),
            pl.BlockSpec((None, 3, t, t), lambda h, i: (h, 0, 0, 0)),
            pl.BlockSpec((None, 8, LANE), lambda h, i: (layer, 0, 0)),
            pl.BlockSpec((None, 1, LANE), lambda h, i: (layer, 0, 0)),
        ],
        out_specs=pl.BlockSpec((None, s, LANE), lambda h, i: (i, 0, h)),
        out_shape=jax.ShapeDtypeStruct((b, s, DIFF_HEADS * HEAD_DIM), _BF16),
        compiler_params=_cparams(("parallel", "parallel")),
        name="diff_attn",
    )(main4, main4, main4, bias, lam_rows, subln_g)


def _gelu_tanh(x):
    return 0.5 * x * (1.0 + jnp.tanh(math.sqrt(2.0 / math.pi) * (x + 0.044715 * (x * x * x))))


def _nsa_compress_kernel(k_ref, v_ref, pos_ref, wk1_ref, wk2_ref, wv1_ref, wv2_ref, kc_ref, vc_ref,
                         stage_ref, rows_ref, *, n_cmp):
    half = CMP_STRIDE * HEAD_DIM
    pos = pos_ref[...]
    rows = rows_ref.shape[0]
    keep = lax.broadcasted_iota(jnp.int32, (rows, HEAD_DIM), 0) < n_cmp

    def compress(x_ref, w1_ref, w2_ref, out_ref):
        stage_ref[...] = x_ref[...].astype(_F32)
        for l in range(CMP_STRIDE):
            rows_ref[:, l * HEAD_DIM:(l + 1) * HEAD_DIM] = (
                stage_ref[pl.ds(l, rows, stride=CMP_STRIDE), :].astype(_BF16))
        r = rows_ref[...]
        first = _dot(r, w1_ref[:half, :])
        second = _dot(r, w1_ref[half:, :])
        pre = first + pltpu.roll(second, rows - 1, 0) + _dot(pos, w1_ref[...])
        out = _dot(_gelu_tanh(pre).astype(_BF16), w2_ref[...])
        out_ref[...] = jnp.where(keep, out, 0.0).astype(out_ref.dtype)

    compress(k_ref, wk1_ref, wk2_ref, kc_ref)
    compress(v_ref, wv1_ref, wv2_ref, vc_ref)


def _nsa_compress(main4, pos_flat, wk1, wk2, wv1, wv2, layer, n_cmp):
    _, b, s, _ = main4.shape
    rows = s // CMP_STRIDE
    lay = lambda a: pl.BlockSpec((None,) + a.shape[1:], lambda i, j: (layer,) + (0,) * (a.ndim - 1))
    out_blk = pl.BlockSpec((None, None, rows, HEAD_DIM), lambda i, j: (i, j, 0, 0))
    out_sd = jax.ShapeDtypeStruct((b, NSA_KV_GROUPS, rows, HEAD_DIM), _BF16)
    return pl.pallas_call(
        functools.partial(_nsa_compress_kernel, n_cmp=n_cmp),
        grid=(b, NSA_KV_GROUPS),
        in_specs=[
            _cols(s, C_NKC, LANE, lambda i, j: (i, 0, j)),
            _cols(s, C_NVC, LANE, lambda i, j: (i, 0, j)),
            lay(pos_flat), lay(wk1), lay(wk2), lay(wv1), lay(wv2),
        ],
        out_specs=[out_blk, out_blk],
        out_shape=[out_sd, out_sd],
        scratch_shapes=[pltpu.VMEM((s, HEAD_DIM), _F32), pltpu.VMEM((rows, CMP_STRIDE * HEAD_DIM), _BF16)],
        compiler_params=_cparams(("parallel", "parallel")),
        name="nsa_compress",
    )(main4, main4, pos_flat, wk1, wk2, wv1, wv2)


def _nsa_attn_kernel(q_ref, kc_ref, vc_ref, ks_ref, vs_ref, kw_ref, vw_ref, gate_ref, toep_ref, cmpb_ref,
                     ovt_ref, exp_ref, o_ref, *, t):
    for qi in range(ks_ref.shape[0] // t):
        pl.when(pl.program_id(2) == qi)(functools.partial(
            _nsa_query_tile, qi, q_ref, kc_ref, vc_ref, ks_ref, vs_ref, kw_ref, vw_ref, gate_ref, toep_ref,
            cmpb_ref, ovt_ref, exp_ref, o_ref, t))


def _nsa_query_tile(qi, q_ref, kc_ref, vc_ref, ks_ref, vs_ref, kw_ref, vw_ref, gate_ref, toep_ref, cmpb_ref,
                    ovt_ref, exp_ref, o_ref, t):
    t0 = qi * t
    tile = lambda i: slice(i * t, (i + 1) * t)
    n_sel = ovt_ref.shape[0]
    qh = [q_ref[:, h * HEAD_DIM:(h + 1) * HEAD_DIM] for h in range(NSA_HPG)]

    tpos = t0 + lax.broadcasted_iota(jnp.int32, (t, 1), 0)
    row_ok = jnp.where(tpos >= CMP_LEN - 1, 1.0, 0.0)
    kc = kc_ref[...]
    vc = vc_ref[...]
    o_cmp = []
    psum = jnp.zeros((t, LANE), _F32)
    for h in range(NSA_HPG):
        sc = _dot_nt(qh[h], kc) + cmpb_ref[h]
        e = jnp.exp2(sc - jnp.max(sc, axis=-1, keepdims=True))
        p = e / jnp.sum(e, axis=-1, keepdims=True) * row_ok
        o_cmp.append(_dot(p.astype(_BF16), vc))
        psum = psum + p

    ovt = ovt_ref[...]
    p_hi, p_mid, p_lo = _split3(psum)
    imp = (_dot_nt(ovt, p_hi) + _dot_nt(ovt, p_mid)) + _dot_nt(ovt, p_lo)
    j = lax.broadcasted_iota(jnp.int32, (n_sel, t), 0)
    blk_t = (t0 + lax.broadcasted_iota(jnp.int32, (n_sel, t), 1)) // SEL_LEN
    valid = j <= blk_t
    forced = (j == 0) | (j == blk_t) | (j == blk_t - 1)
    score = jnp.where(valid, imp + jnp.where(forced, FORCE_SCORE, 0.0), NEG_INF)
    rank = jnp.zeros((n_sel, t), _F32)
    for k in range(n_sel):
        sk = score[k:k + 1, :]
        ahead = (sk > score) | ((sk == score) & (j > k))
        rank = rank + jnp.where(ahead, 1.0, 0.0)
    drop_t = jnp.where(valid & (rank < SEL_TOPN), 0.0, NEG_INF)
    drop_t = jnp.concatenate([drop_t, jnp.zeros((LANE - n_sel, t), _F32)], axis=0)
    drop = jnp.transpose(drop_t).astype(_BF16)

    q_all = jnp.concatenate(qh, axis=0)
    rows = NSA_HPG * t

    def add_bias(s, which):
        return (s.reshape(NSA_HPG, t, t) + toep_ref[which]).reshape(rows, t)

    q_sel = jnp.concatenate([q_all, jnp.concatenate([drop] * NSA_HPG, axis=0)], axis=1)

    state = _softmax_init(rows)
    for kj in range(qi + 1):
        s = _dot_nt(q_sel, jnp.concatenate([ks_ref[tile(kj), :], exp_ref[tile(kj), :]], axis=1))
        if qi - kj < 2:
            s = add_bias(s, qi - kj)
        state = _online_update(s, vs_ref[tile(kj), :], *state)
    o_sel = _softmax_result(state)

    d_edge = WINDOW // t
    state = _softmax_init(rows)
    for kj in range(max(qi - d_edge, 0), qi + 1):
        d = qi - kj
        s = add_bias(_dot_nt(q_all, kw_ref[tile(kj), :]), 2 if d == d_edge else d)
        state = _online_update(s, vw_ref[tile(kj), :], *state)
    o_win = _softmax_result(state)

    gl = gate_ref[...]
    gates = 1.0 / (1.0 + jnp.exp(-gl))
    for h in range(NSA_HPG):
        hs = slice(h * t, (h + 1) * t)
        out = (gates[:, 3 * h:3 * h + 1] * o_cmp[h]
               + gates[:, 3 * h + 1:3 * h + 2] * o_sel[hs]
               + gates[:, 3 * h + 2:3 * h + 3] * o_win[hs])
        o_ref[:, h * HEAD_DIM:(h + 1) * HEAD_DIM] = out.astype(o_ref.dtype)


def _nsa_attn(main4, kcmp, vcmp, gates3, toep, cmpb, ovt, expand, t=NSA_T):
    _, b, s, _ = main4.shape
    gw = NSA_HPG * HEAD_DIM
    kv = lambda c: _cols(s, c, LANE, lambda g, i, q: (i, 0, g))
    cmp_blk = pl.BlockSpec((None, None, LANE, HEAD_DIM), lambda g, i, q: (i, g, 0, 0))
    return pl.pallas_call(
        functools.partial(_nsa_attn_kernel, t=t),
        grid=(NSA_KV_GROUPS, b, s // t),
        in_specs=[
            _cols(t, C_NQ, gw, lambda g, i, q: (i, q, g)),
            cmp_blk, cmp_blk,
            kv(C_NKS), kv(C_NVS), kv(C_NKW), kv(C_NVW),
            pl.BlockSpec((None, t, LANE), lambda g, i, q: (i, q, 1 + g)),
            pl.BlockSpec((None, 3, NSA_HPG, t, t), lambda g, i, q: (g, 0, 0, 0, 0)),
            pl.BlockSpec((NSA_HPG, t, LANE), lambda g, i, q: (g, q, 0)),
            pl.BlockSpec(ovt.shape, lambda g, i, q: (0, 0)),
            pl.BlockSpec(expand.shape, lambda g, i, q: (0, 0)),
        ],
        out_specs=pl.BlockSpec((None, t, gw), lambda g, i, q: (i, q, g)),
        out_shape=jax.ShapeDtypeStruct((b, s, NSA_HEADS * HEAD_DIM), _BF16),
        compiler_params=_cparams(("parallel", "parallel", "arbitrary")),
        name="nsa_attn",
    )(main4, kcmp, vcmp, main4, main4, main4, main4, gates3, toep, cmpb, ovt, expand)


def _outproj_kernel(a1_ref, a2_ref, a3_ref, w_ref, h_ref, o_ref):
    r1 = a1_ref.shape[1]
    r2 = r1 + a2_ref.shape[1]
    o_ref[...] = h_ref[...] + ((_dot(a1_ref[...], w_ref[:r1, :]) + _dot(a2_ref[...], w_ref[r1:r2, :]))
                               + _dot(a3_ref[...], w_ref[r2:, :]))


def _outproj(a1, a2, a3, w, h, layer, tm=512):
    m = h.shape[0]
    act = lambda a: pl.BlockSpec((tm, a.shape[1]), lambda i: (i, 0))
    return pl.pallas_call(
        _outproj_kernel,
        grid=(m // tm,),
        in_specs=[act(a1), act(a2), act(a3),
                  pl.BlockSpec((None, D_MODEL, D_MODEL), lambda i: (layer, 0, 0), pipeline_mode=pl.Buffered(1)),
                  act(h)],
        out_specs=act(h),
        out_shape=jax.ShapeDtypeStruct(h.shape, _F32),
        compiler_params=_cparams(("parallel",)),
        name="outproj_residual",
    )(a1, a2, a3, w, h)


def _ffn_kernel(x_ref, halo_ref, g_ref, wg_ref, wu_ref, cwg_ref, cwu_ref, cbg_ref, cbu_ref, wd_ref, fg_ref, o_ref,
                xn_ref, hg_ref, hu_ref, *, tm, seq, final_norm):
    i = pl.program_id(0)
    j = pl.program_id(1)

    @pl.when(j == 0)
    def _():
        g = g_ref[...]
        x = x_ref[...]
        xn_ref[HALO:, :] = _rms(x, g).astype(_BF16)
        keep = jnp.where((i * tm) % seq == 0, 0.0, 1.0)
        xn_ref[:HALO, :] = (_rms(halo_ref[0], g) * keep).astype(_BF16)
        o_ref[...] = x

    xn = xn_ref[...]
    hg_ref[...] = _dot(xn, wg_ref[...])
    hu_ref[...] = _dot(xn, wu_ref[...])

    def conv(h_ref, cw_ref, cb_ref):
        out = cb_ref[...] + cw_ref[CONV_W - 1:CONV_W, :] * h_ref[HALO:, :]
        for tap in range(CONV_W - 1):
            shift = CONV_W - 1 - tap
            out = out + cw_ref[tap:tap + 1, :] * h_ref[HALO - shift:HALO - shift + tm, :]
        return out

    gate = conv(hg_ref, cwg_ref, cbg_ref)
    up = conv(hu_ref, cwu_ref, cbu_ref)
    act = gate / (1.0 + jnp.exp(-gate)) * up
    o_ref[...] += _dot(act.astype(_BF16), wd_ref[...])

    if final_norm:
        @pl.when(j == pl.num_programs(1) - 1)
        def _():
            o_ref[...] = _rms(o_ref[...], fg_ref[...])


def _ffn(h, g, w_up, conv_w, conv_b, w_down, final_g, layer, seq, final_norm, tm=512, tf=512):
    m = h.shape[0]
    nf = D_FF // tf
    halo_view = h.reshape(m // HALO, HALO, D_MODEL)
    lay = lambda shape, imap: pl.BlockSpec((None,) + shape, lambda i, j: (layer,) + imap(i, j))
    return pl.pallas_call(
        functools.partial(_ffn_kernel, tm=tm, seq=seq, final_norm=final_norm),
        grid=(m // tm, nf),
        in_specs=[
            pl.BlockSpec((tm, D_MODEL), lambda i, j: (i, 0)),
            pl.BlockSpec((1, HALO, D_MODEL), lambda i, j: (jnp.maximum(i * (tm // HALO) - 1, 0), 0, 0)),
            lay((1, D_MODEL), lambda i, j: (0, 0)),
            lay((D_MODEL, tf), lambda i, j: (0, j)),
            lay((D_MODEL, tf), lambda i, j: (0, j + nf)),
            lay((CONV_W, tf), lambda i, j: (0, j)),
            lay((CONV_W, tf), lambda i, j: (0, j + nf)),
            lay((1, tf), lambda i, j: (0, j)),
            lay((1, tf), lambda i, j: (0, j + nf)),
            lay((tf, D_MODEL), lambda i, j: (j, 0)),
            pl.BlockSpec((1, D_MODEL), lambda i, j: (0, 0)),
        ],
        out_specs=pl.BlockSpec((tm, D_MODEL), lambda i, j: (i, 0)),
        out_shape=jax.ShapeDtypeStruct(h.shape, _F32),
        scratch_shapes=[
            pltpu.VMEM((tm + HALO, D_MODEL), _BF16),
            pltpu.VMEM((tm + HALO, tf), _F32),
            pltpu.VMEM((tm + HALO, tf), _F32),
        ],
        compiler_params=_cparams(("parallel", "arbitrary")),
        name="rms_conv_ffn",
    )(h, halo_view, g, w_up, w_up, conv_w, conv_w, conv_b, conv_b, w_down, final_g)


def _pad_lanes(v, width=LANE):
    return jnp.pad(v, ((0, 0), (0, width - v.shape[-1])))


def kernel(x, attn_norm_g, w_in, fox_f_bias, diff_lq1, diff_lk1, diff_lq2, diff_lk2, diff_subln_g, nsa_cmp_pos, nsa_cmp_wk1, nsa_cmp_wk2, nsa_cmp_wv1, nsa_cmp_wv2, w_out, ffn_norm_g, ffn_w_up, ffn_conv_w, ffn_conv_b, ffn_w_down, rel_bias, final_norm_g):
    b, s, _ = x.shape
    depth = w_in.shape[0]
    m = b * s
    t = ATT_T
    n_cmp = (s - CMP_LEN) // CMP_STRIDE + 1
    n_sel = s // SEL_LEN
    assert n_cmp + 1 == s // CMP_STRIDE == LANE and n_sel >= SEL_TOPN and s % t == 0

    w_main = jnp.concatenate([w_in[:, :, o:o + n] for o, n in _MAIN_SEGMENTS], axis=2).astype(_BF16)
    pad = lambda n: jnp.zeros((depth, D_MODEL, LANE - n), w_in.dtype)
    per_group = NSA_HPG * 3
    w_gate = jnp.concatenate(
        [w_in[:, :, _O_FF:_O_FF + FOX_HEADS], pad(FOX_HEADS)]
        + [a for g in range(NSA_KV_GROUPS)
           for a in (w_in[:, :, _O_NG + g * per_group:_O_NG + (g + 1) * per_group], pad(per_group))],
        axis=2).astype(_BF16)
    w_out_b = w_out.astype(_BF16)
    w_up_b = ffn_w_up.astype(_BF16)
    w_down_b = ffn_w_down.astype(_BF16)
    wk1, wk2 = nsa_cmp_wk1.astype(_BF16), nsa_cmp_wk2.astype(_BF16)
    wv1, wv2 = nsa_cmp_wv1.astype(_BF16), nsa_cmp_wv2.astype(_BF16)
    pos_flat = nsa_cmp_pos.reshape(depth, 1, CMP_LEN * HEAD_DIM).astype(_BF16)
    f_bias_rows = _pad_lanes(fox_f_bias)[:, None, :]
    lam_rows = jnp.stack([_pad_lanes(v) for v in (diff_lq1, diff_lk1, diff_lq2, diff_lk2)], axis=1)
    lam_rows = jnp.pad(lam_rows, ((0, 0), (0, 4), (0, 0)))
    attn_g, ffn_g, subln_g = attn_norm_g[:, None, :], ffn_norm_g[:, None, :], diff_subln_g[:, None, :]
    conv_b = ffn_conv_b[:, None, :]

    toep_idx = lambda tt: _toeplitz_bucket_tiles(tt).reshape(3 * tt, tt)
    diff_bias = _bias_tables(rel_bias, toep_idx(t), DIFF_HEADS, 0).reshape(DIFF_HEADS, 3, t, t)
    nsa_bias = _bias_tables(rel_bias, toep_idx(NSA_T), NSA_HEADS, DIFF_HEADS).reshape(
        NSA_KV_GROUPS, NSA_HPG, 3, NSA_T, NSA_T)
    nsa_bias = jnp.transpose(nsa_bias, (0, 2, 1, 3, 4))
    cmp_bias = _bias_tables(rel_bias, _cmp_bucket_tile(s), NSA_HEADS, DIFF_HEADS)
    cmp_starts = np.arange(LANE) * CMP_STRIDE
    sel_starts = np.arange(n_sel) * SEL_LEN
    overlap = np.clip(np.minimum(cmp_starts[:, None] + CMP_LEN, sel_starts[None, :] + SEL_LEN)
                      - np.maximum(cmp_starts[:, None], sel_starts[None, :]), 0, None).astype(np.float32) / CMP_LEN
    overlap[n_cmp:] = 0.0
    ovt = jnp.asarray(overlap.T, _BF16)
    expand = np.zeros((s, LANE), np.float32)
    expand[np.arange(s), np.arange(s) // SEL_LEN] = 1.0
    expand = jnp.asarray(expand, _BF16)

    h = x.reshape(m, D_MODEL)
    for l in range(depth):
        main, gates = _inproj(h, attn_g, w_main, w_gate, l)
        main4 = main.reshape(N_CHUNK, b, s, CHUNK_W)
        gates3 = gates.reshape(b, s, N_GATE)

        qa, ka = _fox_prep(gates3, f_bias_rows, l)
        fox_o = _fox_attn(main4, qa, ka)

        lam_init = 0.8 - 0.6 * math.exp(-0.3 * l)
        diff_o = _diff_attn(main4, diff_bias, lam_rows, subln_g, l, lam_init)

        kcmp, vcmp = _nsa_compress(main4, pos_flat, wk1, wk2, wv1, wv2, l, n_cmp)
        nsa_o = _nsa_attn(main4, kcmp, vcmp, gates3, nsa_bias, cmp_bias, ovt, expand)

        h = _outproj(fox_o.reshape(m, -1), diff_o.reshape(m, -1), nsa_o.reshape(m, -1), w_out_b, h, l)
        h = _ffn(h, ffn_g, w_up_b, ffn_conv_w, conv_b, w_down_b, final_norm_g[None], l, s,
                 final_norm=(l == depth - 1))
    return h.reshape(b, s, D_MODEL)
```

```python
import functools
import math

import jax
import jax.numpy as jnp
import numpy as np
from jax import lax
from jax.experimental import pallas as pl
from jax.experimental.pallas import tpu as pltpu

D_MODEL = 2048
HEAD_DIM = 128
FOX_HEADS = 6
DIFF_HEADS = 4
DIFF_QK_DIM = HEAD_DIM // 2
NSA_HEADS = 6
NSA_KV_GROUPS = 2
NSA_HPG = NSA_HEADS // NSA_KV_GROUPS
CMP_LEN = 32
CMP_STRIDE = 16
SEL_LEN = 64
SEL_TOPN = 8
WINDOW = 512
N_BUCKETS = 32
MAX_DISTANCE = 128
D_FF = 5632
CONV_W = 3
EPS = 1e-6
NEG_INF = -1e30
FORCE_SCORE = 1e4

IN_SIZES = (
    FOX_HEADS * HEAD_DIM, FOX_HEADS * HEAD_DIM, FOX_HEADS * HEAD_DIM, FOX_HEADS,
    DIFF_HEADS * 2 * DIFF_QK_DIM, DIFF_HEADS * 2 * DIFF_QK_DIM, DIFF_HEADS * HEAD_DIM,
    NSA_HEADS * HEAD_DIM,
    NSA_KV_GROUPS * HEAD_DIM, NSA_KV_GROUPS * HEAD_DIM,
    NSA_KV_GROUPS * HEAD_DIM, NSA_KV_GROUPS * HEAD_DIM,
    NSA_KV_GROUPS * HEAD_DIM, NSA_KV_GROUPS * HEAD_DIM,
    NSA_HEADS * 3,
)
_IN_OFF = np.concatenate([[0], np.cumsum(IN_SIZES)])
(_O_FQ, _O_FK, _O_FV, _O_FF, _O_DQ, _O_DK, _O_DV, _O_NQ, _O_NKC, _O_NVC, _O_NKS, _O_NVS,
 _O_NKW, _O_NVW, _O_NG) = [int(v) for v in _IN_OFF[:-1]]

LANE = 128
HALO = 8
C_FQ, C_FK, C_FV, C_NQ = 0, 768, 1536, 2304
C_DQ, C_DK, C_DV = 3072, 3584, 4096
C_NKC, C_NVC, C_NKS, C_NVS, C_NKW, C_NVW = 4608, 4864, 5120, 5376, 5632, 5888
N_MAIN = 6144
CHUNK_W = 768
N_CHUNK = N_MAIN // CHUNK_W
N_GATE = 3 * LANE
ATT_T = 256
NSA_T = 512
LOG2E = math.log2(math.e)
VMEM_LIMIT = 56 * 1024 * 1024

_F32 = jnp.float32
_BF16 = jnp.bfloat16


_MAIN_SEGMENTS = ((_O_FQ, 768), (_O_FK, 768), (_O_FV, 768), (_O_NQ, 768), (_O_DQ, 512), (_O_DK, 512),
                  (_O_DV, 512), (_O_NKC, 256), (_O_NVC, 256), (_O_NKS, 256), (_O_NVS, 256),
                  (_O_NKW, 256), (_O_NVW, 256))


def _t5_bucket_np(rel):
    n = np.maximum(rel, 0)
    max_exact = N_BUCKETS // 2
    nf = np.maximum(n, 1).astype(np.float32)
    large = max_exact + (np.log(nf / np.float32(max_exact)) / np.float32(math.log(MAX_DISTANCE / max_exact))
                         * np.float32(N_BUCKETS - max_exact)).astype(np.int32)
    large = np.minimum(large, N_BUCKETS - 1)
    return np.where(n < max_exact, n, large).astype(np.int32)


def _toeplitz_bucket_tiles(t):
    i = np.arange(t)[:, None]
    j = np.arange(t)[None, :]
    d0 = np.where(j <= i, _t5_bucket_np(i - j), -1)
    d1 = _t5_bucket_np(i - j + t)
    assert np.all(_t5_bucket_np(np.arange(t + 1, 8 * t)) == N_BUCKETS - 1)
    assert WINDOW in (t, 2 * t)
    edge = np.where(i < j, _t5_bucket_np(i - j + WINDOW), -1)
    return np.stack([d0, d1, edge]).astype(np.int32)


def _cmp_bucket_tile(s):
    n = np.arange(LANE)[None, :]
    t = np.arange(s)[:, None]
    n_cmp = (s - CMP_LEN) // CMP_STRIDE + 1
    block_end = n * CMP_STRIDE + CMP_LEN - 1
    ok = (block_end <= t) & (n < n_cmp)
    return np.where(ok, _t5_bucket_np(t - block_end), -1).astype(np.int32)


def _dot(a, b):
    return jnp.dot(a, b, preferred_element_type=_F32)


def _dot_nt(a, b):
    return lax.dot_general(a, b, (((1,), (1,)), ((), ())), preferred_element_type=_F32)


def _split3(x):
    hi = x.astype(_BF16)
    r = x - hi.astype(_F32)
    mid = r.astype(_BF16)
    lo = (r - mid.astype(_F32)).astype(_BF16)
    return hi, mid, lo


def _rms(x, g):
    return x * lax.rsqrt(jnp.mean(x * x, axis=-1, keepdims=True) + EPS) * g


def _cparams(sem, vmem=VMEM_LIMIT):
    return pltpu.CompilerParams(dimension_semantics=sem, vmem_limit_bytes=vmem)


def _cols(rows, col, width, where):
    per = CHUNK_W // width

    def index_map(*grid):
        batch, row_blk, off = where(*grid)
        blk = col // width + off
        return blk // per, batch, row_blk, blk % per

    return pl.BlockSpec((None, None, rows, width), index_map)


def _inproj_kernel(x_ref, g_ref, w_ref, wg_ref, cs_ref, main_ref, gate_ref, xn_ref):
    xn_ref[...] = _rms(x_ref[...], g_ref[...]).astype(_BF16)
    gate_ref[...] = _dot(xn_ref[...], wg_ref[...])
    for c in range(N_CHUNK):
        w = w_ref[:, c * CHUNK_W:(c + 1) * CHUNK_W]
        main_ref[c] = (_dot(xn_ref[...], w) * cs_ref[c]).astype(_BF16)


def _query_col_scale():
    cs = np.ones((1, N_MAIN), np.float32)
    cs[:, C_FQ:C_FQ + FOX_HEADS * HEAD_DIM] = HEAD_DIM ** -0.5 * LOG2E
    cs[:, C_NQ:C_NQ + NSA_HEADS * HEAD_DIM] = HEAD_DIM ** -0.5 * LOG2E
    cs[:, C_DQ:C_DQ + DIFF_HEADS * 2 * DIFF_QK_DIM] = DIFF_QK_DIM ** -0.5 * LOG2E
    return cs.reshape(N_CHUNK, 1, CHUNK_W)


def _inproj(h, g, w_main, w_gate, layer, tm=256):
    m = h.shape[0]
    once = pl.Buffered(1)
    return pl.pallas_call(
        _inproj_kernel,
        grid=(m // tm,),
        in_specs=[
            pl.BlockSpec((tm, D_MODEL), lambda i: (i, 0)),
            pl.BlockSpec((None, 1, D_MODEL), lambda i: (layer, 0, 0)),
            pl.BlockSpec((None, D_MODEL, N_MAIN), lambda i: (layer, 0, 0), pipeline_mode=once),
            pl.BlockSpec((None, D_MODEL, N_GATE), lambda i: (layer, 0, 0), pipeline_mode=once),
            pl.BlockSpec((N_CHUNK, 1, CHUNK_W), lambda i: (0, 0, 0), pipeline_mode=once),
        ],
        out_specs=[
            pl.BlockSpec((N_CHUNK, tm, CHUNK_W), lambda i: (0, i, 0)),
            pl.BlockSpec((tm, N_GATE), lambda i: (i, 0)),
        ],
        out_shape=[jax.ShapeDtypeStruct((N_CHUNK, m, CHUNK_W), _BF16), jax.ShapeDtypeStruct((m, N_GATE), _F32)],
        scratch_shapes=[pltpu.VMEM((tm, D_MODEL), _BF16)],
        compiler_params=_cparams(("parallel",)),
        name="rms_inproj",
    )(h, g, w_main, w_gate, jnp.asarray(_query_col_scale()))


def _bias_table_kernel(rel_ref, idx_ref, out_ref):
    h = pl.program_id(0)
    idx = idx_ref[...]
    acc = jnp.full(idx.shape, NEG_INF, _F32)
    last = rel_ref[N_BUCKETS - 1, h]
    for b in range(N_BUCKETS):
        acc = jnp.where(idx == b, (rel_ref[b, h] - last) * LOG2E, acc)
    out_ref[...] = acc


def _bias_tables(rel_bias, idx, n_heads, head0):
    r, c = idx.shape
    return pl.pallas_call(
        _bias_table_kernel,
        grid=(n_heads,),
        in_specs=[
            pl.BlockSpec(memory_space=pltpu.SMEM),
            pl.BlockSpec((r, c), lambda h: (0, 0)),
        ],
        out_specs=pl.BlockSpec((None, r, c), lambda h: (h, 0, 0)),
        out_shape=jax.ShapeDtypeStruct((n_heads, r, c), _F32),
        compiler_params=_cparams(("arbitrary",)),
        name="t5_bias_tables",
    )(rel_bias[:, head0:head0 + n_heads], jnp.asarray(idx))


def _fox_prep_kernel(gate_ref, fb_ref, eq_ref, ek_ref, cq_ref, ck_ref, qa_ref, ka_ref, *, chunk):
    s = gate_ref.shape[0]
    row = lax.broadcasted_iota(jnp.int32, (chunk, chunk), 0)
    col = lax.broadcasted_iota(jnp.int32, (chunk, chunk), 1)
    tri = jnp.where(col <= row, 1.0, 0.0).astype(_BF16)
    carry = jnp.zeros((1, LANE), _F32)
    for c in range(s // chunk):
        rows = slice(c * chunk, (c + 1) * chunk)
        x = gate_ref[rows, :] + fb_ref[...]
        log_f = jnp.minimum(x, 0.0) - jnp.log1p(jnp.exp(-jnp.abs(x)))
        hi, mid, lo = _split3(log_f)
        cum = (_dot(tri, hi) + _dot(tri, mid)) + _dot(tri, lo) + carry
        carry = cum[chunk - 1:chunk, :]
        parts = jnp.concatenate(_split3(cum * LOG2E), axis=1)
        qa_ref[rows, :] = (_dot(parts, eq_ref[...]) + cq_ref[...]).astype(_BF16)
        ka_ref[rows, :] = (_dot(parts, ek_ref[...]) + ck_ref[...]).astype(_BF16)


def _fox_scatter_constants():
    w = FOX_HEADS * LANE
    eq, ek = np.zeros((3 * LANE, w), np.float32), np.zeros((3 * LANE, w), np.float32)
    cq, ck = np.zeros((1, w), np.float32), np.zeros((1, w), np.float32)
    for h in range(FOX_HEADS):
        for part in range(3):
            eq[part * LANE + h, h * LANE + part] = 1.0
            ek[part * LANE + h, h * LANE + 3 + part] = -1.0
        cq[0, h * LANE + 3:h * LANE + 6] = 1.0
        ck[0, h * LANE:h * LANE + 3] = 1.0
    return jnp.asarray(eq, _BF16), jnp.asarray(ek, _BF16), jnp.asarray(cq), jnp.asarray(ck)


def _fox_prep(gates3, f_bias_rows, layer):
    b, s, _ = gates3.shape
    w = FOX_HEADS * LANE
    consts = _fox_scatter_constants()
    return pl.pallas_call(
        functools.partial(_fox_prep_kernel, chunk=256),
        grid=(b,),
        in_specs=[
            pl.BlockSpec((None, s, LANE), lambda i: (i, 0, 0)),
            pl.BlockSpec((None, 1, LANE), lambda i: (layer, 0, 0)),
        ] + [pl.BlockSpec(c.shape, lambda i: (0, 0)) for c in consts],
        out_specs=[
            pl.BlockSpec((None, s, w), lambda i: (i, 0, 0)),
            pl.BlockSpec((None, s, w), lambda i: (i, 0, 0)),
        ],
        out_shape=[jax.ShapeDtypeStruct((b, s, w), _BF16), jax.ShapeDtypeStruct((b, s, w), _BF16)],
        compiler_params=_cparams(("parallel",)),
        name="fox_prep",
    )(gates3, f_bias_rows, *consts)


def _online_update(s, v, m, l, acc):
    m_new = jnp.maximum(m, jnp.max(s, axis=-1, keepdims=True))
    alpha = jnp.exp2(m - m_new)
    p = jnp.exp2(s - m_new)
    part = p[:, :LANE]
    for c in range(1, p.shape[1] // LANE):
        part = part + p[:, c * LANE:(c + 1) * LANE]
    l_new = alpha * l + part
    acc_new = alpha * acc + _dot(p.astype(_BF16), v)
    return m_new, l_new, acc_new


def _softmax_init(rows):
    return (jnp.full((rows, 1), NEG_INF, _F32), jnp.zeros((rows, LANE), _F32), jnp.zeros((rows, HEAD_DIM), _F32))


def _softmax_result(state):
    _, l, acc = state
    return acc / jnp.sum(l, axis=-1, keepdims=True)


def _fox_attn_kernel(q_ref, k_ref, v_ref, qa_ref, ka_ref, o_ref, *, t):
    n_q = q_ref.shape[0] // t
    tile = lambda i: slice(i * t, (i + 1) * t)
    row = lax.broadcasted_iota(jnp.int32, (t, t), 0)
    col = lax.broadcasted_iota(jnp.int32, (t, t), 1)
    for qi in range(n_q):
        q = jnp.concatenate([q_ref[tile(qi), :], qa_ref[tile(qi), :]], axis=1)
        state = _softmax_init(t)
        for kj in range(qi + 1):
            s = _dot_nt(q, jnp.concatenate([k_ref[tile(kj), :], ka_ref[tile(kj), :]], axis=1))
            if kj == qi:
                s = jnp.where(col <= row, s, NEG_INF)
            state = _online_update(s, v_ref[tile(kj), :], *state)
        o_ref[tile(qi), :] = _softmax_result(state).astype(o_ref.dtype)


def _fox_attn(main4, qa, ka, t=ATT_T):
    _, b, s, _ = main4.shape
    return pl.pallas_call(
        functools.partial(_fox_attn_kernel, t=t),
        grid=(b, FOX_HEADS),
        in_specs=[
            _cols(s, C_FQ, LANE, lambda i, h: (i, 0, h)),
            _cols(s, C_FK, LANE, lambda i, h: (i, 0, h)),
            _cols(s, C_FV, LANE, lambda i, h: (i, 0, h)),
            pl.BlockSpec((None, s, LANE), lambda i, h: (i, 0, h)),
            pl.BlockSpec((None, s, LANE), lambda i, h: (i, 0, h)),
        ],
        out_specs=pl.BlockSpec((None, s, LANE), lambda i, h: (i, 0, h)),
        out_shape=jax.ShapeDtypeStruct((b, s, FOX_HEADS * HEAD_DIM), _BF16),
        compiler_params=_cparams(("parallel", "parallel")),
        name="fox_attn",
    )(main4, main4, main4, qa, ka)


def _diff_attn_kernel(q_ref, k_ref, v_ref, bias_ref, lam_ref, g_ref, o_ref, *, t, lam_init):
    n_q = q_ref.shape[0] // t
    tile = lambda i: slice(i * t, (i + 1) * t)
    lane = lax.broadcasted_iota(jnp.int32, (1, LANE), 1)
    first_map = jnp.where(lane < DIFF_QK_DIM, 1.0, 0.0).astype(q_ref.dtype)
    lv = lam_ref[...]
    lam = (jnp.exp(jnp.sum(lv[0:1, :] * lv[1:2, :], axis=-1, keepdims=True))
           - jnp.exp(jnp.sum(lv[2:3, :] * lv[3:4, :], axis=-1, keepdims=True)) + lam_init)
    for qi in reversed(range(n_q)):
        q = q_ref[tile(qi), :]
        q1 = q * first_map
        q_both = jnp.concatenate([q1, q - q1], axis=0)
        state = _softmax_init(2 * t)
        for kj in range(qi + 1):
            s = _dot_nt(q_both, k_ref[tile(kj), :])
            if qi - kj < 2:
                s = (s.reshape(2, t, t) + bias_ref[qi - kj][None]).reshape(2 * t, t)
            state = _online_update(s, v_ref[tile(kj), :], *state)
        a = _softmax_result(state)
        o = a[:t] - lam * a[t:]
        o_ref[tile(qi), :] = (_rms(o, g_ref[...]) * (1.0 - lam_init)).astype(o_ref.dtype)


def _diff_attn(main4, bias, lam_rows, subln_g, layer, lam_init, t=ATT_T):
    _, b, s, _ = main4.shape
    return pl.pallas_call(
        functools.partial(_diff_attn_kernel, t=t, lam_init=lam_init),
        grid=(DIFF_HEADS, b),
        in_specs=[
            _cols(s, C_DQ, LANE, lambda h, i: (i, 0, h)),
            _cols(s, C_DK, LANE, lambda h, i: (i, 0, h)),
            _cols(s, C_DV, LANE, lambda h, i: (i, 0, h)),
            pl.BlockSpec((None, 3, t, t), lambda h, i: (h, 0, 0, 0)),
            pl.BlockSpec((None, 8, LANE), lambda h, i: (layer, 0, 0)),
            pl.BlockSpec((None, 1, LANE), lambda h, i: (layer, 0, 0)),
        ],
        out_specs=pl.BlockSpec((None, s, LANE), lambda h, i: (i, 0, h)),
        out_shape=jax.ShapeDtypeStruct((b, s, DIFF_HEADS * HEAD_DIM), _BF16),
        compiler_params=_cparams(("parallel", "parallel")),
        name="diff_attn",
    )(main4, main4, main4, bias, lam_rows, subln_g)


def _gelu_tanh(x):
    return 0.5 * x * (1.0 + jnp.tanh(math.sqrt(2.0 / math.pi) * (x + 0.044715 * (x * x * x))))


def _nsa_compress_kernel(k_ref, v_ref, pos_ref, wk1_ref, wk2_ref, wv1_ref, wv2_ref, kc_ref, vc_ref,
                         stage_ref, rows_ref, *, n_cmp):
    half = CMP_STRIDE * HEAD_DIM
    pos = pos_ref[...]
    rows = rows_ref.shape[0]
    keep = lax.broadcasted_iota(jnp.int32, (rows, HEAD_DIM), 0) < n_cmp

    def compress(x_ref, w1_ref, w2_ref, out_ref):
        stage_ref[...] = x_ref[...].astype(_F32)
        for l in range(CMP_STRIDE):
            rows_ref[:, l * HEAD_DIM:(l + 1) * HEAD_DIM] = (
                stage_ref[pl.ds(l, rows, stride=CMP_STRIDE), :].astype(_BF16))
        r = rows_ref[...]
        first = _dot(r, w1_ref[:half, :])
        second = _dot(r, w1_ref[half:, :])
        pre = first + pltpu.roll(second, rows - 1, 0) + _dot(pos, w1_ref[...])
        out = _dot(_gelu_tanh(pre).astype(_BF16), w2_ref[...])
        out_ref[...] = jnp.where(keep, out, 0.0).astype(out_ref.dtype)

    compress(k_ref, wk1_ref, wk2_ref, kc_ref)
    compress(v_ref, wv1_ref, wv2_ref, vc_ref)


def _nsa_compress(main4, pos_flat, wk1, wk2, wv1, wv2, layer, n_cmp):
    _, b, s, _ = main4.shape
    rows = s // CMP_STRIDE
    lay = lambda a: pl.BlockSpec((None,) + a.shape[1:], lambda i, j: (layer,) + (0,) * (a.ndim - 1))
    out_blk = pl.BlockSpec((None, None, rows, HEAD_DIM), lambda i, j: (i, j, 0, 0))
    out_sd = jax.ShapeDtypeStruct((b, NSA_KV_GROUPS, rows, HEAD_DIM), _BF16)
    return pl.pallas_call(
        functools.partial(_nsa_compress_kernel, n_cmp=n_cmp),
        grid=(b, NSA_KV_GROUPS),
        in_specs=[
            _cols(s, C_NKC, LANE, lambda i, j: (i, 0, j)),
            _cols(s, C_NVC, LANE, lambda i, j: (i, 0, j)),
            lay(pos_flat), lay(wk1), lay(wk2), lay(wv1), lay(wv2),
        ],
        out_specs=[out_blk, out_blk],
        out_shape=[out_sd, out_sd],
        scratch_shapes=[pltpu.VMEM((s, HEAD_DIM), _F32), pltpu.VMEM((rows, CMP_STRIDE * HEAD_DIM), _BF16)],
        compiler_params=_cparams(("parallel", "parallel")),
        name="nsa_compress",
    )(main4, main4, pos_flat, wk1, wk2, wv1, wv2)


def _nsa_attn_kernel(q_ref, kc_ref, vc_ref, ks_ref, vs_ref, kw_ref, vw_ref, gate_ref, toep_ref, cmpb_ref,
                     ovt_ref, exp_ref, o_ref, *, t):
    for qi in range(ks_ref.shape[0] // t):
        pl.when(pl.program_id(2) == qi)(functools.partial(
            _nsa_query_tile, qi, q_ref, kc_ref, vc_ref, ks_ref, vs_ref, kw_ref, vw_ref, gate_ref, toep_ref,
            cmpb_ref, ovt_ref, exp_ref, o_ref, t))


def _nsa_query_tile(qi, q_ref, kc_ref, vc_ref, ks_ref, vs_ref, kw_ref, vw_ref, gate_ref, toep_ref, cmpb_ref,
                    ovt_ref, exp_ref, o_ref, t):
    t0 = qi * t
    tile = lambda i: slice(i * t, (i + 1) * t)
    n_sel = ovt_ref.shape[0]
    qh = [q_ref[:, h * HEAD_DIM:(h + 1) * HEAD_DIM] for h in range(NSA_HPG)]

    tpos = t0 + lax.broadcasted_iota(jnp.int32, (t, 1), 0)
    row_ok = jnp.where(tpos >= CMP_LEN - 1, 1.0, 0.0)
    kc = kc_ref[...]
    vc = vc_ref[...]
    o_cmp = []
    psum = jnp.zeros((t, LANE), _F32)
    for h in range(NSA_HPG):
        sc = _dot_nt(qh[h], kc) + cmpb_ref[h]
        e = jnp.exp2(sc - jnp.max(sc, axis=-1, keepdims=True))
        p = e / jnp.sum(e, axis=-1, keepdims=True) * row_ok
        o_cmp.append(_dot(p.astype(_BF16), vc))
        psum = psum + p

    ovt = ovt_ref[...]
    p_hi, p_mid, p_lo = _split3(psum)
    imp = (_dot_nt(ovt, p_hi) + _dot_nt(ovt, p_mid)) + _dot_nt(ovt, p_lo)
    j = lax.broadcasted_iota(jnp.int32, (n_sel, t), 0)
    blk_t = (t0 + lax.broadcasted_iota(jnp.int32, (n_sel, t), 1)) // SEL_LEN
    valid = j <= blk_t
    forced = (j == 0) | (j == blk_t) | (j == blk_t - 1)
    score = jnp.where(valid, imp + jnp.where(forced, FORCE_SCORE, 0.0), NEG_INF)
    rank = jnp.zeros((n_sel, t), _F32)
    for k in range(n_sel):
        sk = score[k:k + 1, :]
        ahead = (sk > score) | ((sk == score) & (j > k))
        rank = rank + jnp.where(ahead, 1.0, 0.0)
    drop_t = jnp.where(valid & (rank < SEL_TOPN), 0.0, NEG_INF)
    drop_t = jnp.concatenate([drop_t, jnp.zeros((LANE - n_sel, t), _F32)], axis=0)
    drop = jnp.transpose(drop_t).astype(_BF16)

    q_all = jnp.concatenate(qh, axis=0)
    rows = NSA_HPG * t

    def add_bias(s, which):
        return (s.reshape(NSA_HPG, t, t) + toep_ref[which]).reshape(rows, t)

    q_sel = jnp.concatenate([q_all, jnp.concatenate([drop] * NSA_HPG, axis=0)], axis=1)

    state = _softmax_init(rows)
    for kj in range(qi + 1):
        s = _dot_nt(q_sel, jnp.concatenate([ks_ref[tile(kj), :], exp_ref[tile(kj), :]], axis=1))
        if qi - kj < 2:
            s = add_bias(s, qi - kj)
        state = _online_update(s, vs_ref[tile(kj), :], *state)
    o_sel = _softmax_result(state)

    d_edge = WINDOW // t
    state = _softmax_init(rows)
    for kj in range(max(qi - d_edge, 0), qi + 1):
        d = qi - kj
        s = add_bias(_dot_nt(q_all, kw_ref[tile(kj), :]), 2 if d == d_edge else d)
        state = _online_update(s, vw_ref[tile(kj), :], *state)
    o_win = _softmax_result(state)

    gl = gate_ref[...]
    gates = 1.0 / (1.0 + jnp.exp(-gl))
    for h in range(NSA_HPG):
        hs = slice(h * t, (h + 1) * t)
        out = (gates[:, 3 * h:3 * h + 1] * o_cmp[h]
               + gates[:, 3 * h + 1:3 * h + 2] * o_sel[hs]
               + gates[:, 3 * h + 2:3 * h + 3] * o_win[hs])
        o_ref[:, h * HEAD_DIM:(h + 1) * HEAD_DIM] = out.astype(o_ref.dtype)


def _nsa_attn(main4, kcmp, vcmp, gates3, toep, cmpb, ovt, expand, t=NSA_T):
    _, b, s, _ = main4.shape
    gw = NSA_HPG * HEAD_DIM
    kv = lambda c: _cols(s, c, LANE, lambda g, i, q: (i, 0, g))
    cmp_blk = pl.BlockSpec((None, None, LANE, HEAD_DIM), lambda g, i, q: (i, g, 0, 0))
    return pl.pallas_call(
        functools.partial(_nsa_attn_kernel, t=t),
        grid=(NSA_KV_GROUPS, b, s // t),
        in_specs=[
            _cols(t, C_NQ, gw, lambda g, i, q: (i, q, g)),
            cmp_blk, cmp_blk,
            kv(C_NKS), kv(C_NVS), kv(C_NKW), kv(C_NVW),
            pl.BlockSpec((None, t, LANE), lambda g, i, q: (i, q, 1 + g)),
            pl.BlockSpec((None, 3, NSA_HPG, t, t), lambda g, i, q: (g, 0, 0, 0, 0)),
            pl.BlockSpec((NSA_HPG, t, LANE), lambda g, i, q: (g, q, 0)),
            pl.BlockSpec(ovt.shape, lambda g, i, q: (0, 0)),
            pl.BlockSpec(expand.shape, lambda g, i, q: (0, 0)),
        ],
        out_specs=pl.BlockSpec((None, t, gw), lambda g, i, q: (i, q, g)),
        out_shape=jax.ShapeDtypeStruct((b, s, NSA_HEADS * HEAD_DIM), _BF16),
        compiler_params=_cparams(("parallel", "parallel", "arbitrary")),
        name="nsa_attn",
    )(main4, kcmp, vcmp, main4, main4, main4, main4, gates3, toep, cmpb, ovt, expand)


def _outproj_kernel(a1_ref, a2_ref, a3_ref, w_ref, h_ref, o_ref):
    r1 = a1_ref.shape[1]
    r2 = r1 + a2_ref.shape[1]
    o_ref[...] = h_ref[...] + ((_dot(a1_ref[...], w_ref[:r1, :]) + _dot(a2_ref[...], w_ref[r1:r2, :]))
                               + _dot(a3_ref[...], w_ref[r2:, :]))


def _outproj(a1, a2, a3, w, h, layer, tm=512):
    m = h.shape[0]
    act = lambda a: pl.BlockSpec((tm, a.shape[1]), lambda i: (i, 0))
    return pl.pallas_call(
        _outproj_kernel,
        grid=(m // tm,),
        in_specs=[act(a1), act(a2), act(a3),
                  pl.BlockSpec((None, D_MODEL, D_MODEL), lambda i: (layer, 0, 0), pipeline_mode=pl.Buffered(1)),
                  act(h)],
        out_specs=act(h),
        out_shape=jax.ShapeDtypeStruct(h.shape, _F32),
        compiler_params=_cparams(("parallel",)),
        name="outproj_residual",
    )(a1, a2, a3, w, h)


def _ffn_kernel(x_ref, halo_ref, g_ref, wg_ref, wu_ref, cwg_ref, cwu_ref, cbg_ref, cbu_ref, wd_ref, fg_ref, o_ref,
                xn_ref, hg_ref, hu_ref, *, tm, seq, final_norm):
    i = pl.program_id(0)
    j = pl.program_id(1)

    @pl.when(j == 0)
    def _():
        g = g_ref[...]
        x = x_ref[...]
        xn_ref[HALO:, :] = _rms(x, g).astype(_BF16)
        keep = jnp.where((i * tm) % seq == 0, 0.0, 1.0)
        xn_ref[:HALO, :] = (_rms(halo_ref[0], g) * keep).astype(_BF16)
        o_ref[...] = x

    xn = xn_ref[...]
    hg_ref[...] = _dot(xn, wg_ref[...])
    hu_ref[...] = _dot(xn, wu_ref[...])

    def conv(h_ref, cw_ref, cb_ref):
        out = cb_ref[...] + cw_ref[CONV_W - 1:CONV_W, :] * h_ref[HALO:, :]
        for tap in range(CONV_W - 1):
            shift = CONV_W - 1 - tap
            out = out + cw_ref[tap:tap + 1, :] * h_ref[HALO - shift:HALO - shift + tm, :]
        return out

    gate = conv(hg_ref, cwg_ref, cbg_ref)
    up = conv(hu_ref, cwu_ref, cbu_ref)
    act = gate / (1.0 + jnp.exp(-gate)) * up
    o_ref[...] += _dot(act.astype(_BF16), wd_ref[...])

    if final_norm:
        @pl.when(j == pl.num_programs(1) - 1)
        def _():
            o_ref[...] = _rms(o_ref[...], fg_ref[...])


def _ffn(h, g, w_up, conv_w, conv_b, w_down, final_g, layer, seq, final_norm, tm=512, tf=512):
    m = h.shape[0]
    nf = D_FF // tf
    halo_view = h.reshape(m // HALO, HALO, D_MODEL)
    lay = lambda shape, imap: pl.BlockSpec((None,) + shape, lambda i, j: (layer,) + imap(i, j))
    return pl.pallas_call(
        functools.partial(_ffn_kernel, tm=tm, seq=seq, final_norm=final_norm),
        grid=(m // tm, nf),
        in_specs=[
            pl.BlockSpec((tm, D_MODEL), lambda i, j: (i, 0)),
            pl.BlockSpec((1, HALO, D_MODEL), lambda i, j: (jnp.maximum(i * (tm // HALO) - 1, 0), 0, 0)),
            lay((1, D_MODEL), lambda i, j: (0, 0)),
            lay((D_MODEL, tf), lambda i, j: (0, j)),
            lay((D_MODEL, tf), lambda i, j: (0, j + nf)),
            lay((CONV_W, tf), lambda i, j: (0, j)),
            lay((CONV_W, tf), lambda i, j: (0, j + nf)),
            lay((1, tf), lambda i, j: (0, j)),
            lay((1, tf), lambda i, j: (0, j + nf)),
            lay((tf, D_MODEL), lambda i, j: (j, 0)),
            pl.BlockSpec((1, D_MODEL), lambda i, j: (0, 0)),
        ],
        out_specs=pl.BlockSpec((tm, D_MODEL), lambda i, j: (i, 0)),
        out_shape=jax.ShapeDtypeStruct(h.shape, _F32),
        scratch_shapes=[
            pltpu.VMEM((tm + HALO, D_MODEL), _BF16),
            pltpu.VMEM((tm + HALO, tf), _F32),
            pltpu.VMEM((tm + HALO, tf), _F32),
        ],
        compiler_params=_cparams(("parallel", "arbitrary")),
        name="rms_conv_ffn",
    )(h, halo_view, g, w_up, w_up, conv_w, conv_w, conv_b, conv_b, w_down, final_g)


def _pad_lanes(v, width=LANE):
    return jnp.pad(v, ((0, 0), (0, width - v.shape[-1])))


def kernel(x, attn_norm_g, w_in, fox_f_bias, diff_lq1, diff_lk1, diff_lq2, diff_lk2, diff_subln_g, nsa_cmp_pos, nsa_cmp_wk1, nsa_cmp_wk2, nsa_cmp_wv1, nsa_cmp_wv2, w_out, ffn_norm_g, ffn_w_up, ffn_conv_w, ffn_conv_b, ffn_w_down, rel_bias, final_norm_g):
    b, s, _ = x.shape
    depth = w_in.shape[0]
    m = b * s
    t = ATT_T
    n_cmp = (s - CMP_LEN) // CMP_STRIDE + 1
    n_sel = s // SEL_LEN
    assert n_cmp + 1 == s // CMP_STRIDE == LANE and n_sel >= SEL_TOPN and s % t == 0

    w_main = jnp.concatenate([w_in[:, :, o:o + n] for o, n in _MAIN_SEGMENTS], axis=2).astype(_BF16)
    pad = lambda n: jnp.zeros((depth, D_MODEL, LANE - n), w_in.dtype)
    per_group = NSA_HPG * 3
    w_gate = jnp.concatenate(
        [w_in[:, :, _O_FF:_O_FF + FOX_HEADS], pad(FOX_HEADS)]
        + [a for g in range(NSA_KV_GROUPS)
           for a in (w_in[:, :, _O_NG + g * per_group:_O_NG + (g + 1) * per_group], pad(per_group))],
        axis=2).astype(_BF16)
    w_out_b = w_out.astype(_BF16)
    w_up_b = ffn_w_up.astype(_BF16)
    w_down_b = ffn_w_down.astype(_BF16)
    wk1, wk2 = nsa_cmp_wk1.astype(_BF16), nsa_cmp_wk2.astype(_BF16)
    wv1, wv2 = nsa_cmp_wv1.astype(_BF16), nsa_cmp_wv2.astype(_BF16)
    pos_flat = nsa_cmp_pos.reshape(depth, 1, CMP_LEN * HEAD_DIM).astype(_BF16)
    f_bias_rows = _pad_lanes(fox_f_bias)[:, None, :]
    lam_rows = jnp.stack([_pad_lanes(v) for v in (diff_lq1, diff_lk1, diff_lq2, diff_lk2)], axis=1)
    lam_rows = jnp.pad(lam_rows, ((0, 0), (0, 4), (0, 0)))
    attn_g, ffn_g, subln_g = attn_norm_g[:, None, :], ffn_norm_g[:, None, :], diff_subln_g[:, None, :]
    conv_b = ffn_conv_b[:, None, :]

    toep_idx = lambda tt: _toeplitz_bucket_tiles(tt).reshape(3 * tt, tt)
    diff_bias = _bias_tables(rel_bias, toep_idx(t), DIFF_HEADS, 0).reshape(DIFF_HEADS, 3, t, t)
    nsa_bias = _bias_tables(rel_bias, toep_idx(NSA_T), NSA_HEADS, DIFF_HEADS).reshape(
        NSA_KV_GROUPS, NSA_HPG, 3, NSA_T, NSA_T)
    nsa_bias = jnp.transpose(nsa_bias, (0, 2, 1, 3, 4))
    cmp_bias = _bias_tables(rel_bias, _cmp_bucket_tile(s), NSA_HEADS, DIFF_HEADS)
    cmp_starts = np.arange(LANE) * CMP_STRIDE
    sel_starts = np.arange(n_sel) * SEL_LEN
    overlap = np.clip(np.minimum(cmp_starts[:, None] + CMP_LEN, sel_starts[None, :] + SEL_LEN)
                      - np.maximum(cmp_starts[:, None], sel_starts[None, :]), 0, None).astype(np.float32) / CMP_LEN
    overlap[n_cmp:] = 0.0
    ovt = jnp.asarray(overlap.T, _BF16)
    expand = np.zeros((s, LANE), np.float32)
    expand[np.arange(s), np.arange(s) // SEL_LEN] = 1.0
    expand = jnp.asarray(expand, _BF16)

    h = x.reshape(m, D_MODEL)
    for l in range(depth):
        main, gates = _inproj(h, attn_g, w_main, w_gate, l)
        main4 = main.reshape(N_CHUNK, b, s, CHUNK_W)
        gates3 = gates.reshape(b, s, N_GATE)

        qa, ka = _fox_prep(gates3, f_bias_rows, l)
        fox_o = _fox_attn(main4, qa, ka)

        lam_init = 0.8 - 0.6 * math.exp(-0.3 * l)
        diff_o = _diff_attn(main4, diff_bias, lam_rows, subln_g, l, lam_init)

        kcmp, vcmp = _nsa_compress(main4, pos_flat, wk1, wk2, wv1, wv2, l, n_cmp)
        nsa_o = _nsa_attn(main4, kcmp, vcmp, gates3, nsa_bias, cmp_bias, ovt, expand)

        h = _outproj(fox_o.reshape(m, -1), diff_o.reshape(m, -1), nsa_o.reshape(m, -1), w_out_b, h, l)
        h = _ffn(h, ffn_g, w_up_b, ffn_conv_w, conv_b, w_down_b, final_norm_g[None], l, s,
                 final_norm=(l == depth - 1))
    return h.reshape(b, s, D_MODEL)
```

```python
import functools
import math

import jax
import jax.numpy as jnp
import numpy as np
from jax import lax
from jax.experimental import pallas as pl
from jax.experimental.pallas import tpu as pltpu

D_MODEL = 2048
HEAD_DIM = 128
FOX_HEADS = 6
DIFF_HEADS = 4
DIFF_QK_DIM = HEAD_DIM // 2
NSA_HEADS = 6
NSA_KV_GROUPS = 2
NSA_HPG = NSA_HEADS // NSA_KV_GROUPS
CMP_LEN = 32
CMP_STRIDE = 16
SEL_LEN = 64
SEL_TOPN = 8
WINDOW = 512
N_BUCKETS = 32
MAX_DISTANCE = 128
D_FF = 5632
CONV_W = 3
EPS = 1e-6
NEG_INF = -1e30
FORCE_SCORE = 1e4

IN_SIZES = (
    FOX_HEADS * HEAD_DIM, FOX_HEADS * HEAD_DIM, FOX_HEADS * HEAD_DIM, FOX_HEADS,
    DIFF_HEADS * 2 * DIFF_QK_DIM, DIFF_HEADS * 2 * DIFF_QK_DIM, DIFF_HEADS * HEAD_DIM,
    NSA_HEADS * HEAD_DIM,
    NSA_KV_GROUPS * HEAD_DIM, NSA_KV_GROUPS * HEAD_DIM,
    NSA_KV_GROUPS * HEAD_DIM, NSA_KV_GROUPS * HEAD_DIM,
    NSA_KV_GROUPS * HEAD_DIM, NSA_KV_GROUPS * HEAD_DIM,
    NSA_HEADS * 3,
)
_IN_OFF = np.concatenate([[0], np.cumsum(IN_SIZES)])
(_O_FQ, _O_FK, _O_FV, _O_FF, _O_DQ, _O_DK, _O_DV, _O_NQ, _O_NKC, _O_NVC, _O_NKS, _O_NVS,
 _O_NKW, _O_NVW, _O_NG) = [int(v) for v in _IN_OFF[:-1]]

LANE = 128
HALO = 8
C_FQ, C_FK, C_FV, C_NQ = 0, 768, 1536, 2304
C_DQ, C_DK, C_DV = 3072, 3584, 4096
C_NKC, C_NVC, C_NKS, C_NVS, C_NKW, C_NVW = 4608, 4864, 5120, 5376, 5632, 5888
N_MAIN = 6144
CHUNK_W = 768
N_CHUNK = N_MAIN // CHUNK_W
N_GATE = 3 * LANE
ATT_T = 256
NSA_T = 512
LOG2E = math.log2(math.e)
VMEM_LIMIT = 56 * 1024 * 1024

_F32 = jnp.float32
_BF16 = jnp.bfloat16


_MAIN_SEGMENTS = ((_O_FQ, 768), (_O_FK, 768), (_O_FV, 768), (_O_NQ, 768), (_O_DQ, 512), (_O_DK, 512),
                  (_O_DV, 512), (_O_NKC, 256), (_O_NVC, 256), (_O_NKS, 256), (_O_NVS, 256),
                  (_O_NKW, 256), (_O_NVW, 256))


def _t5_bucket_np(rel):
    n = np.maximum(rel, 0)
    max_exact = N_BUCKETS // 2
    nf = np.maximum(n, 1).astype(np.float32)
    large = max_exact + (np.log(nf / np.float32(max_exact)) / np.float32(math.log(MAX_DISTANCE / max_exact))
                         * np.float32(N_BUCKETS - max_exact)).astype(np.int32)
    large = np.minimum(large, N_BUCKETS - 1)
    return np.where(n < max_exact, n, large).astype(np.int32)


def _toeplitz_bucket_tiles(t):
    i = np.arange(t)[:, None]
    j = np.arange(t)[None, :]
    d0 = np.where(j <= i, _t5_bucket_np(i - j), -1)
    d1 = _t5_bucket_np(i - j + t)
    assert np.all(_t5_bucket_np(np.arange(t + 1, 8 * t)) == N_BUCKETS - 1)
    assert WINDOW in (t, 2 * t)
    edge = np.where(i < j, _t5_bucket_np(i - j + WINDOW), -1)
    return np.stack([d0, d1, edge]).astype(np.int32)


def _cmp_bucket_tile(s):
    n = np.arange(LANE)[None, :]
    t = np.arange(s)[:, None]
    n_cmp = (s - CMP_LEN) // CMP_STRIDE + 1
    block_end = n * CMP_STRIDE + CMP_LEN - 1
    ok = (block_end <= t) & (n < n_cmp)
    return np.where(ok, _t5_bucket_np(t - block_end), -1).astype(np.int32)


def _dot(a, b):
    return jnp.dot(a, b, preferred_element_type=_F32)


def _dot_nt(a, b):
    return lax.dot_general(a, b, (((1,), (1,)), ((), ())), preferred_element_type=_F32)


def _split3(x):
    hi = x.astype(_BF16)
    r = x - hi.astype(_F32)
    mid = r.astype(_BF16)
    lo = (r - mid.astype(_F32)).astype(_BF16)
    return hi, mid, lo


def _rms(x, g):
    return x * lax.rsqrt(jnp.mean(x * x, axis=-1, keepdims=True) + EPS) * g


def _cparams(sem, vmem=VMEM_LIMIT):
    return pltpu.CompilerParams(dimension_semantics=sem, vmem_limit_bytes=vmem)


def _cols(rows, col, width, where):
    per = CHUNK_W // width

    def index_map(*grid):
        batch, row_blk, off = where(*grid)
        blk = col // width + off
        return blk // per, batch, row_blk, blk % per

    return pl.BlockSpec((None, None, rows, width), index_map)


def _inproj_kernel(x_ref, g_ref, w_ref, wg_ref, cs_ref, main_ref, gate_ref, xn_ref):
    xn_ref[...] = _rms(x_ref[...], g_ref[...]).astype(_BF16)
    gate_ref[...] = _dot(xn_ref[...], wg_ref[...])
    for c in range(N_CHUNK):
        w = w_ref[:, c * CHUNK_W:(c + 1) * CHUNK_W]
        main_ref[c] = (_dot(xn_ref[...], w) * cs_ref[c]).astype(_BF16)


def _query_col_scale():
    cs = np.ones((1, N_MAIN), np.float32)
    cs[:, C_FQ:C_FQ + FOX_HEADS * HEAD_DIM] = HEAD_DIM ** -0.5 * LOG2E
    cs[:, C_NQ:C_NQ + NSA_HEADS * HEAD_DIM] = HEAD_DIM ** -0.5 * LOG2E
    cs[:, C_DQ:C_DQ + DIFF_HEADS * 2 * DIFF_QK_DIM] = DIFF_QK_DIM ** -0.5 * LOG2E
    return cs.reshape(N_CHUNK, 1, CHUNK_W)


def _inproj(h, g, w_main, w_gate, layer, tm=256):
    m = h.shape[0]
    once = pl.Buffered(1)
    return pl.pallas_call(
        _inproj_kernel,
        grid=(m // tm,),
        in_specs=[
            pl.BlockSpec((tm, D_MODEL), lambda i: (i, 0)),
            pl.BlockSpec((None, 1, D_MODEL), lambda i: (layer, 0, 0)),
            pl.BlockSpec((None, D_MODEL, N_MAIN), lambda i: (layer, 0, 0), pipeline_mode=once),
            pl.BlockSpec((None, D_MODEL, N_GATE), lambda i: (layer, 0, 0), pipeline_mode=once),
            pl.BlockSpec((N_CHUNK, 1, CHUNK_W), lambda i: (0, 0, 0), pipeline_mode=once),
        ],
        out_specs=[
            pl.BlockSpec((N_CHUNK, tm, CHUNK_W), lambda i: (0, i, 0)),
            pl.BlockSpec((tm, N_GATE), lambda i: (i, 0)),
        ],
        out_shape=[jax.ShapeDtypeStruct((N_CHUNK, m, CHUNK_W), _BF16), jax.ShapeDtypeStruct((m, N_GATE), _F32)],
        scratch_shapes=[pltpu.VMEM((tm, D_MODEL), _BF16)],
        compiler_params=_cparams(("parallel",)),
        name="rms_inproj",
    )(h, g, w_main, w_gate, jnp.asarray(_query_col_scale()))


def _bias_table_kernel(rel_ref, idx_ref, out_ref):
    h = pl.program_id(0)
    idx = idx_ref[...]
    acc = jnp.full(idx.shape, NEG_INF, _F32)
    last = rel_ref[N_BUCKETS - 1, h]
    for b in range(N_BUCKETS):
        acc = jnp.where(idx == b, (rel_ref[b, h] - last) * LOG2E, acc)
    out_ref[...] = acc


def _bias_tables(rel_bias, idx, n_heads, head0):
    r, c = idx.shape
    return pl.pallas_call(
        _bias_table_kernel,
        grid=(n_heads,),
        in_specs=[
            pl.BlockSpec(memory_space=pltpu.SMEM),
            pl.BlockSpec((r, c), lambda h: (0, 0)),
        ],
        out_specs=pl.BlockSpec((None, r, c), lambda h: (h, 0, 0)),
        out_shape=jax.ShapeDtypeStruct((n_heads, r, c), _F32),
        compiler_params=_cparams(("arbitrary",)),
        name="t5_bias_tables",
    )(rel_bias[:, head0:head0 + n_heads], jnp.asarray(idx))


def _fox_prep_kernel(gate_ref, fb_ref, eq_ref, ek_ref, cq_ref, ck_ref, qa_ref, ka_ref, *, chunk):
    s = gate_ref.shape[0]
    row = lax.broadcasted_iota(jnp.int32, (chunk, chunk), 0)
    col = lax.broadcasted_iota(jnp.int32, (chunk, chunk), 1)
    tri = jnp.where(col <= row, 1.0, 0.0).astype(_BF16)
    carry = jnp.zeros((1, LANE), _F32)
    for c in range(s // chunk):
        rows = slice(c * chunk, (c + 1) * chunk)
        x = gate_ref[rows, :] + fb_ref[...]
        log_f = jnp.minimum(x, 0.0) - jnp.log1p(jnp.exp(-jnp.abs(x)))
        hi, mid, lo = _split3(log_f)
        cum = (_dot(tri, hi) + _dot(tri, mid)) + _dot(tri, lo) + carry
        carry = cum[chunk - 1:chunk, :]
        parts = jnp.concatenate(_split3(cum * LOG2E), axis=1)
        qa_ref[rows, :] = (_dot(parts, eq_ref[...]) + cq_ref[...]).astype(_BF16)
        ka_ref[rows, :] = (_dot(parts, ek_ref[...]) + ck_ref[...]).astype(_BF16)


def _fox_scatter_constants():
    w = FOX_HEADS * LANE
    eq, ek = np.zeros((3 * LANE, w), np.float32), np.zeros((3 * LANE, w), np.float32)
    cq, ck = np.zeros((1, w), np.float32), np.zeros((1, w), np.float32)
    for h in range(FOX_HEADS):
        for part in range(3):
            eq[part * LANE + h, h * LANE + part] = 1.0
            ek[part * LANE + h, h * LANE + 3 + part] = -1.0
        cq[0, h * LANE + 3:h * LANE + 6] = 1.0
        ck[0, h * LANE:h * LANE + 3] = 1.0
    return jnp.asarray(eq, _BF16), jnp.asarray(ek, _BF16), jnp.asarray(cq), jnp.asarray(ck)


def _fox_prep(gates3, f_bias_rows, layer):
    b, s, _ = gates3.shape
    w = FOX_HEADS * LANE
    consts = _fox_scatter_constants()
    return pl.pallas_call(
        functools.partial(_fox_prep_kernel, chunk=256),
        grid=(b,),
        in_specs=[
            pl.BlockSpec((None, s, LANE), lambda i: (i, 0, 0)),
            pl.BlockSpec((None, 1, LANE), lambda i: (layer, 0, 0)),
        ] + [pl.BlockSpec(c.shape, lambda i: (0, 0)) for c in consts],
        out_specs=[
            pl.BlockSpec((None, s, w), lambda i: (i, 0, 0)),
            pl.BlockSpec((None, s, w), lambda i: (i, 0, 0)),
        ],
        out_shape=[jax.ShapeDtypeStruct((b, s, w), _BF16), jax.ShapeDtypeStruct((b, s, w), _BF16)],
        compiler_params=_cparams(("parallel",)),
        name="fox_prep",
    )(gates3, f_bias_rows, *consts)


def _online_update(s, v, m, l, acc):
    m_new = jnp.maximum(m, jnp.max(s, axis=-1, keepdims=True))
    alpha = jnp.exp2(m - m_new)
    p = jnp.exp2(s - m_new)
    part = p[:, :LANE]
    for c in range(1, p.shape[1] // LANE):
        part = part + p[:, c * LANE:(c + 1) * LANE]
    l_new = alpha * l + part
    acc_new = alpha * acc + _dot(p.astype(_BF16), v)
    return m_new, l_new, acc_new


def _softmax_init(rows):
    return (jnp.full((rows, 1), NEG_INF, _F32), jnp.zeros((rows, LANE), _F32), jnp.zeros((rows, HEAD_DIM), _F32))


def _softmax_result(state):
    _, l, acc = state
    return acc / jnp.sum(l, axis=-1, keepdims=True)


def _fox_attn_kernel(q_ref, k_ref, v_ref, qa_ref, ka_ref, o_ref, *, t):
    n_q = q_ref.shape[0] // t
    tile = lambda i: slice(i * t, (i + 1) * t)
    row = lax.broadcasted_iota(jnp.int32, (t, t), 0)
    col = lax.broadcasted_iota(jnp.int32, (t, t), 1)
    for qi in range(n_q):
        q = jnp.concatenate([q_ref[tile(qi), :], qa_ref[tile(qi), :]], axis=1)
        state = _softmax_init(t)
        for kj in range(qi + 1):
            s = _dot_nt(q, jnp.concatenate([k_ref[tile(kj), :], ka_ref[tile(kj), :]], axis=1))
            if kj == qi:
                s = jnp.where(col <= row, s, NEG_INF)
            state = _online_update(s, v_ref[tile(kj), :], *state)
        o_ref[tile(qi), :] = _softmax_result(state).astype(o_ref.dtype)


def _fox_attn(main4, qa, ka, t=ATT_T):
    _, b, s, _ = main4.shape
    return pl.pallas_call(
        functools.partial(_fox_attn_kernel, t=t),
        grid=(b, FOX_HEADS),
        in_specs=[
            _cols(s, C_FQ, LANE, lambda i, h: (i, 0, h)),
            _cols(s, C_FK, LANE, lambda i, h: (i, 0, h)),
            _cols(s, C_FV, LANE, lambda i, h: (i, 0, h)),
            pl.BlockSpec((None, s, LANE), lambda i, h: (i, 0, h)),
            pl.BlockSpec((None, s, LANE), lambda i, h: (i, 0, h)),
        ],
        out_specs=pl.BlockSpec((None, s, LANE), lambda i, h: (i, 0, h)),
        out_shape=jax.ShapeDtypeStruct((b, s, FOX_HEADS * HEAD_DIM), _BF16),
        compiler_params=_cparams(("parallel", "parallel")),
        name="fox_attn",
    )(main4, main4, main4, qa, ka)


def _diff_attn_kernel(q_ref, k_ref, v_ref, bias_ref, lam_ref, g_ref, o_ref, *, t, lam_init):
    n_q = q_ref.shape[0] // t
    tile = lambda i: slice(i * t, (i + 1) * t)
    lane = lax.broadcasted_iota(jnp.int32, (1, LANE), 1)
    first_map = jnp.where(lane < DIFF_QK_DIM, 1.0, 0.0).astype(q_ref.dtype)
    lv = lam_ref[...]
    lam = (jnp.exp(jnp.sum(lv[0:1, :] * lv[1:2, :], axis=-1, keepdims=True))
           - jnp.exp(jnp.sum(lv[2:3, :] * lv[3:4, :], axis=-1, keepdims=True)) + lam_init)
    for qi in reversed(range(n_q)):
        q = q_ref[tile(qi), :]
        q1 = q * first_map
        q_both = jnp.concatenate([q1, q - q1], axis=0)
        state = _softmax_init(2 * t)
        for kj in range(qi + 1):
            s = _dot_nt(q_both, k_ref[tile(kj), :])
            if qi - kj < 2:
                s = (s.reshape(2, t, t) + bias_ref[qi - kj][None]).reshape(2 * t, t)
            state = _online_update(s, v_ref[tile(kj), :], *state)
        a = _softmax_result(state)
        o = a[:t] - lam * a[t:]
        o_ref[tile(qi), :] = (_rms(o, g_ref[...]) * (1.0 - lam_init)).astype(o_ref.dtype)


def _diff_attn(main4, bias, lam_rows, subln_g, layer, lam_init, t=ATT_T):
    _, b, s, _ = main4.shape
    return pl.pallas_call(
        functools.partial(_diff_attn_kernel, t=t, lam_init=lam_init),
        grid=(DIFF_HEADS, b),
        in_specs=[
            _cols(s, C_DQ, LANE, lambda h, i: (i, 0, h)),
            _cols(s, C_DK, LANE, lambda h, i: (i, 0, h)),
            _cols(s, C_DV, LANE, lambda h, i: (i, 0, h)),
            pl.BlockSpec((None, 3, t, t), lambda h, i: (h, 0, 0, 0)),
            pl.BlockSpec((None, 8, LANE), lambda h, i: (layer, 0, 0)),
            pl.BlockSpec((None, 1, LANE), lambda h, i: (layer, 0, 0)),
        ],
        out_specs=pl.BlockSpec((None, s, LANE), lambda h, i: (i, 0, h)),
        out_shape=jax.ShapeDtypeStruct((b, s, DIFF_HEADS * HEAD_DIM), _BF16),
        compiler_params=_cparams(("parallel", "parallel")),
        name="diff_attn",
    )(main4, main4, main4, bias, lam_rows, subln_g)


def _gelu_tanh(x):
    return 0.5 * x * (1.0 + jnp.tanh(math.sqrt(2.0 / math.pi) * (x + 0.044715 * (x * x * x))))


def _nsa_compress_kernel(k_ref, v_ref, pos_ref, wk1_ref, wk2_ref, wv1_ref, wv2_ref, kc_ref, vc_ref,
                         stage_ref, rows_ref, *, n_cmp):
    half = CMP_STRIDE * HEAD_DIM
    pos = pos_ref[...]
    rows = rows_ref.shape[0]
    keep = lax.broadcasted_iota(jnp.int32, (rows, HEAD_DIM), 0) < n_cmp

    def compress(x_ref, w1_ref, w2_ref, out_ref):
        stage_ref[...] = x_ref[...].astype(_F32)
        for l in range(CMP_STRIDE):
            rows_ref[:, l * HEAD_DIM:(l + 1) * HEAD_DIM] = (
                stage_ref[pl.ds(l, rows, stride=CMP_STRIDE), :].astype(_BF16))
        r = rows_ref[...]
        first = _dot(r, w1_ref[:half, :])
        second = _dot(r, w1_ref[half:, :])
        pre = first + pltpu.roll(second, rows - 1, 0) + _dot(pos, w1_ref[...])
        out = _dot(_gelu_tanh(pre).astype(_BF16), w2_ref[...])
        out_ref[...] = jnp.where(keep, out, 0.0).astype(out_ref.dtype)

    compress(k_ref, wk1_ref, wk2_ref, kc_ref)
    compress(v_ref, wv1_ref, wv2_ref, vc_ref)


def _nsa_compress(main4, pos_flat, wk1, wk2, wv1, wv2, layer, n_cmp):
    _, b, s, _ = main4.shape
    rows = s // CMP_STRIDE
    lay = lambda a: pl.BlockSpec((None,) + a.shape[1:], lambda i, j: (layer,) + (0,) * (a.ndim - 1))
    out_blk = pl.BlockSpec((None, None, rows, HEAD_DIM), lambda i, j: (i, j, 0, 0))
    out_sd = jax.ShapeDtypeStruct((b, NSA_KV_GROUPS, rows, HEAD_DIM), _BF16)
    return pl.pallas_call(
        functools.partial(_nsa_compress_kernel, n_cmp=n_cmp),
        grid=(b, NSA_KV_GROUPS),
        in_specs=[
            _cols(s, C_NKC, LANE, lambda i, j: (i, 0, j)),
            _cols(s, C_NVC, LANE, lambda i, j: (i, 0, j)),
            lay(pos_flat), lay(wk1), lay(wk2), lay(wv1), lay(wv2),
        ],
        out_specs=[out_blk, out_blk],
        out_shape=[out_sd, out_sd],
        scratch_shapes=[pltpu.VMEM((s, HEAD_DIM), _F32), pltpu.VMEM((rows, CMP_STRIDE * HEAD_DIM), _BF16)],
        compiler_params=_cparams(("parallel", "parallel")),
        name="nsa_compress",
    )(main4, main4, pos_flat, wk1, wk2, wv1, wv2)


def _nsa_attn_kernel(q_ref, kc_ref, vc_ref, ks_ref, vs_ref, kw_ref, vw_ref, gate_ref, toep_ref, cmpb_ref,
                     ovt_ref, exp_ref, o_ref, *, t):
    for qi in range(ks_ref.shape[0] // t):
        pl.when(pl.program_id(2) == qi)(functools.partial(
            _nsa_query_tile, qi, q_ref, kc_ref, vc_ref, ks_ref, vs_ref, kw_ref, vw_ref, gate_ref, toep_ref,
            cmpb_ref, ovt_ref, exp_ref, o_ref, t))


def _nsa_query_tile(qi, q_ref, kc_ref, vc_ref, ks_ref, vs_ref, kw_ref, vw_ref, gate_ref, toep_ref, cmpb_ref,
                    ovt_ref, exp_ref, o_ref, t):
    t0 = qi * t
    tile = lambda i: slice(i * t, (i + 1) * t)
    n_sel = ovt_ref.shape[0]
    qh = [q_ref[:, h * HEAD_DIM:(h + 1) * HEAD_DIM] for h in range(NSA_HPG)]

    tpos = t0 + lax.broadcasted_iota(jnp.int32, (t, 1), 0)
    row_ok = jnp.where(tpos >= CMP_LEN - 1, 1.0, 0.0)
    kc = kc_ref[...]
    vc = vc_ref[...]
    o_cmp = []
    psum = jnp.zeros((t, LANE), _F32)
    for h in range(NSA_HPG):
        sc = _dot_nt(qh[h], kc) + cmpb_ref[h]
        e = jnp.exp2(sc - jnp.max(sc, axis=-1, keepdims=True))
        p = e / jnp.sum(e, axis=-1, keepdims=True) * row_ok
        o_cmp.append(_dot(p.astype(_BF16), vc))
        psum = psum + p

    ovt = ovt_ref[...]
    p_hi, p_mid, p_lo = _split3(psum)
    imp = (_dot_nt(ovt, p_hi) + _dot_nt(ovt, p_mid)) + _dot_nt(ovt, p_lo)
    j = lax.broadcasted_iota(jnp.int32, (n_sel, t), 0)
    blk_t = (t0 + lax.broadcasted_iota(jnp.int32, (n_sel, t), 1)) // SEL_LEN
    valid = j <= blk_t
    forced = (j == 0) | (j == blk_t) | (j == blk_t - 1)
    score = jnp.where(valid, imp + jnp.where(forced, FORCE_SCORE, 0.0), NEG_INF)
    rank = jnp.zeros((n_sel, t), _F32)
    for k in range(n_sel):
        sk = score[k:k + 1, :]
        ahead = (sk > score) | ((sk == score) & (j > k))
        rank = rank + jnp.where(ahead, 1.0, 0.0)
    drop_t = jnp.where(valid & (rank < SEL_TOPN), 0.0, NEG_INF)
    drop_t = jnp.concatenate([drop_t, jnp.zeros((LANE - n_sel, t), _F32)], axis=0)
    drop = jnp.transpose(drop_t).astype(_BF16)

    q_all = jnp.concatenate(qh, axis=0)
    rows = NSA_HPG * t

    def add_bias(s, which):
        return (s.reshape(NSA_HPG, t, t) + toep_ref[which]).reshape(rows, t)

    q_sel = jnp.concatenate([q_all, jnp.concatenate([drop] * NSA_HPG, axis=0)], axis=1)

    state = _softmax_init(rows)
    for kj in range(qi + 1):
        s = _dot_nt(q_sel, jnp.concatenate([ks_ref[tile(kj), :], exp_ref[tile(kj), :]], axis=1))
        if qi - kj < 2:
            s = add_bias(s, qi - kj)
        state = _online_update(s, vs_ref[tile(kj), :], *state)
    o_sel = _softmax_result(state)

    d_edge = WINDOW // t
    state = _softmax_init(rows)
    for kj in range(max(qi - d_edge, 0), qi + 1):
        d = qi - kj
        s = add_bias(_dot_nt(q_all, kw_ref[tile(kj), :]), 2 if d == d_edge else d)
        state = _online_update(s, vw_ref[tile(kj), :], *state)
    o_win = _softmax_result(state)

    gl = gate_ref[...]
    gates = 1.0 / (1.0 + jnp.exp(-gl))
    for h in range(NSA_HPG):
        hs = slice(h * t, (h + 1) * t)
        out = (gates[:, 3 * h:3 * h + 1] * o_cmp[h]
               + gates[:, 3 * h + 1:3 * h + 2] * o_sel[hs]
               + gates[:, 3 * h + 2:3 * h + 3] * o_win[hs])
        o_ref[:, h * HEAD_DIM:(h + 1) * HEAD_DIM] = out.astype(o_ref.dtype)


def _nsa_attn(main4, kcmp, vcmp, gates3, toep, cmpb, ovt, expand, t=NSA_T):
    _, b, s, _ = main4.shape
    gw = NSA_HPG * HEAD_DIM
    kv = lambda c: _cols(s, c, LANE, lambda g, i, q: (i, 0, g))
    cmp_blk = pl.BlockSpec((None, None, LANE, HEAD_DIM), lambda g, i, q: (i, g, 0, 0))
    return pl.pallas_call(
        functools.partial(_nsa_attn_kernel, t=t),
        grid=(NSA_KV_GROUPS, b, s // t),
        in_specs=[
            _cols(t, C_NQ, gw, lambda g, i, q: (i, q, g)),
            cmp_blk, cmp_blk,
            kv(C_NKS), kv(C_NVS), kv(C_NKW), kv(C_NVW),
            pl.BlockSpec((None, t, LANE), lambda g, i, q: (i, q, 1 + g)),
            pl.BlockSpec((None, 3, NSA_HPG, t, t), lambda g, i, q: (g, 0, 0, 0, 0)),
            pl.BlockSpec((NSA_HPG, t, LANE), lambda g, i, q: (g, q, 0)),
            pl.BlockSpec(ovt.shape, lambda g, i, q: (0, 0)),
            pl.BlockSpec(expand.shape, lambda g, i, q: (0, 0)),
        ],
        out_specs=pl.BlockSpec((None, t, gw), lambda g, i, q: (i, q, g)),
        out_shape=jax.ShapeDtypeStruct((b, s, NSA_HEADS * HEAD_DIM), _BF16),
        compiler_params=_cparams(("parallel", "parallel", "arbitrary")),
        name="nsa_attn",
    )(main4, kcmp, vcmp, main4, main4, main4, main4, gates3, toep, cmpb, ovt, expand)


def _outproj_kernel(a1_ref, a2_ref, a3_ref, w_ref, h_ref, o_ref):
    r1 = a1_ref.shape[1]
    r2 = r1 + a2_ref.shape[1]
    o_ref[...] = h_ref[...] + ((_dot(a1_ref[...], w_ref[:r1, :]) + _dot(a2_ref[...], w_ref[r1:r2, :]))
                               + _dot(a3_ref[...], w_ref[r2:, :]))


def _outproj(a1, a2, a3, w, h, layer, tm=512):
    m = h.shape[0]
    act = lambda a: pl.BlockSpec((tm, a.shape[1]), lambda i: (i, 0))
    return pl.pallas_call(
        _outproj_kernel,
        grid=(m // tm,),
        in_specs=[act(a1), act(a2), act(a3),
                  pl.BlockSpec((None, D_MODEL, D_MODEL), lambda i: (layer, 0, 0), pipeline_mode=pl.Buffered(1)),
                  act(h)],
        out_specs=act(h),
        out_shape=jax.ShapeDtypeStruct(h.shape, _F32),
        compiler_params=_cparams(("parallel",)),
        name="outproj_residual",
    )(a1, a2, a3, w, h)


FFN_SLOTS = 3


def _ffn_kernel(x_ref, halo_ref, g_ref, cw_ref, cb_ref, fg_ref, wup_hbm, wdn_hbm, o_ref,
                xn_ref, hg_ref, hu_ref, wg_buf, wu_buf, wd_buf, sem, *, tm, tf, nf, seq, layer, final_norm):
    i = pl.program_id(0)

    def copies(j):
        slot = j % FFN_SLOTS
        return (pltpu.make_async_copy(wup_hbm.at[layer, :, pl.ds(j * tf, tf)], wg_buf.at[slot], sem.at[0, slot]),
                pltpu.make_async_copy(wup_hbm.at[layer, :, pl.ds((nf + j) * tf, tf)], wu_buf.at[slot],
                                      sem.at[1, slot]),
                pltpu.make_async_copy(wdn_hbm.at[layer, pl.ds(j * tf, tf), :], wd_buf.at[slot], sem.at[2, slot]))

    def start(j):
        for c in copies(j):
            c.start()

    pl.when(i == 0)(functools.partial(start, 0))

    g = g_ref[...]
    x = x_ref[...]
    xn_ref[HALO:, :] = _rms(x, g).astype(_BF16)
    keep = jnp.where((i * tm) % seq == 0, 0.0, 1.0)
    xn_ref[:HALO, :] = (_rms(halo_ref[0], g) * keep).astype(_BF16)
    o_ref[...] = x
    xn = xn_ref[...]

    def conv(h_ref, cols):
        out = cb_ref[:, cols] + cw_ref[CONV_W - 1:CONV_W, cols] * h_ref[HALO:, :]
        for tap in range(CONV_W - 1):
            shift = CONV_W - 1 - tap
            out = out + cw_ref[tap:tap + 1, cols] * h_ref[HALO - shift:HALO - shift + tm, :]
        return out

    for j in range(nf):
        if j + 1 < nf:
            start(j + 1)
        else:
            pl.when(i + 1 < pl.num_programs(0))(functools.partial(start, 0))
        for c in copies(j):
            c.wait()
        slot = j % FFN_SLOTS
        hg, hu = hg_ref.at[j % 2], hu_ref.at[j % 2]
        hg[...] = _dot(xn, wg_buf[slot])
        hu[...] = _dot(xn, wu_buf[slot])
        gate = conv(hg, slice(j * tf, (j + 1) * tf))
        up = conv(hu, slice((nf + j) * tf, (nf + j + 1) * tf))
        act = gate / (1.0 + jnp.exp(-gate)) * up
        o_ref[...] += _dot(act.astype(_BF16), wd_buf[slot])

    if final_norm:
        o_ref[...] = _rms(o_ref[...], fg_ref[...])


def _ffn(h, g, w_up, conv_w, conv_b, w_down, final_g, layer, seq, final_norm, tm=512, tf=512):
    m = h.shape[0]
    nf = D_FF // tf
    halo_view = h.reshape(m // HALO, HALO, D_MODEL)
    lay = lambda shape: pl.BlockSpec((None,) + shape, lambda i: (layer, 0, 0))
    return pl.pallas_call(
        functools.partial(_ffn_kernel, tm=tm, tf=tf, nf=nf, seq=seq, layer=layer, final_norm=final_norm),
        grid=(m // tm,),
        in_specs=[
            pl.BlockSpec((tm, D_MODEL), lambda i: (i, 0)),
            pl.BlockSpec((1, HALO, D_MODEL), lambda i: (jnp.maximum(i * (tm // HALO) - 1, 0), 0, 0)),
            lay((1, D_MODEL)),
            lay((CONV_W, 2 * D_FF)),
            lay((1, 2 * D_FF)),
            pl.BlockSpec((1, D_MODEL), lambda i: (0, 0)),
            pl.BlockSpec(memory_space=pl.ANY),
            pl.BlockSpec(memory_space=pl.ANY),
        ],
        out_specs=pl.BlockSpec((tm, D_MODEL), lambda i: (i, 0)),
        out_shape=jax.ShapeDtypeStruct(h.shape, _F32),
        scratch_shapes=[
            pltpu.VMEM((tm + HALO, D_MODEL), _BF16),
            pltpu.VMEM((2, tm + HALO, tf), _F32),
            pltpu.VMEM((2, tm + HALO, tf), _F32),
            pltpu.VMEM((FFN_SLOTS, D_MODEL, tf), _BF16),
            pltpu.VMEM((FFN_SLOTS, D_MODEL, tf), _BF16),
            pltpu.VMEM((FFN_SLOTS, tf, D_MODEL), _BF16),
            pltpu.SemaphoreType.DMA((3, FFN_SLOTS)),
        ],
        compiler_params=_cparams(("arbitrary",)),
        name="rms_conv_ffn",
    )(h, halo_view, g, conv_w, conv_b, final_g, w_up, w_down)


def _pad_lanes(v, width=LANE):
    return jnp.pad(v, ((0, 0), (0, width - v.shape[-1])))


def kernel(x, attn_norm_g, w_in, fox_f_bias, diff_lq1, diff_lk1, diff_lq2, diff_lk2, diff_subln_g, nsa_cmp_pos, nsa_cmp_wk1, nsa_cmp_wk2, nsa_cmp_wv1, nsa_cmp_wv2, w_out, ffn_norm_g, ffn_w_up, ffn_conv_w, ffn_conv_b, ffn_w_down, rel_bias, final_norm_g):
    b, s, _ = x.shape
    depth = w_in.shape[0]
    m = b * s
    t = ATT_T
    n_cmp = (s - CMP_LEN) // CMP_STRIDE + 1
    n_sel = s // SEL_LEN
    assert n_cmp + 1 == s // CMP_STRIDE == LANE and n_sel >= SEL_TOPN and s % t == 0

    w_main = jnp.concatenate([w_in[:, :, o:o + n] for o, n in _MAIN_SEGMENTS], axis=2).astype(_BF16)
    pad = lambda n: jnp.zeros((depth, D_MODEL, LANE - n), w_in.dtype)
    per_group = NSA_HPG * 3
    w_gate = jnp.concatenate(
        [w_in[:, :, _O_FF:_O_FF + FOX_HEADS], pad(FOX_HEADS)]
        + [a for g in range(NSA_KV_GROUPS)
           for a in (w_in[:, :, _O_NG + g * per_group:_O_NG + (g + 1) * per_group], pad(per_group))],
        axis=2).astype(_BF16)
    w_out_b = w_out.astype(_BF16)
    w_up_b = ffn_w_up.astype(_BF16)
    w_down_b = ffn_w_down.astype(_BF16)
    wk1, wk2 = nsa_cmp_wk1.astype(_BF16), nsa_cmp_wk2.astype(_BF16)
    wv1, wv2 = nsa_cmp_wv1.astype(_BF16), nsa_cmp_wv2.astype(_BF16)
    pos_flat = nsa_cmp_pos.reshape(depth, 1, CMP_LEN * HEAD_DIM).astype(_BF16)
    f_bias_rows = _pad_lanes(fox_f_bias)[:, None, :]
    lam_rows = jnp.stack([_pad_lanes(v) for v in (diff_lq1, diff_lk1, diff_lq2, diff_lk2)], axis=1)
    lam_rows = jnp.pad(lam_rows, ((0, 0), (0, 4), (0, 0)))
    attn_g, ffn_g, subln_g = attn_norm_g[:, None, :], ffn_norm_g[:, None, :], diff_subln_g[:, None, :]
    conv_b = ffn_conv_b[:, None, :]

    toep_idx = lambda tt: _toeplitz_bucket_tiles(tt).reshape(3 * tt, tt)
    diff_bias = _bias_tables(rel_bias, toep_idx(t), DIFF_HEADS, 0).reshape(DIFF_HEADS, 3, t, t)
    nsa_bias = _bias_tables(rel_bias, toep_idx(NSA_T), NSA_HEADS, DIFF_HEADS).reshape(
        NSA_KV_GROUPS, NSA_HPG, 3, NSA_T, NSA_T)
    nsa_bias = jnp.transpose(nsa_bias, (0, 2, 1, 3, 4))
    cmp_bias = _bias_tables(rel_bias, _cmp_bucket_tile(s), NSA_HEADS, DIFF_HEADS)
    cmp_starts = np.arange(LANE) * CMP_STRIDE
    sel_starts = np.arange(n_sel) * SEL_LEN
    overlap = np.clip(np.minimum(cmp_starts[:, None] + CMP_LEN, sel_starts[None, :] + SEL_LEN)
                      - np.maximum(cmp_starts[:, None], sel_starts[None, :]), 0, None).astype(np.float32) / CMP_LEN
    overlap[n_cmp:] = 0.0
    ovt = jnp.asarray(overlap.T, _BF16)
    expand = np.zeros((s, LANE), np.float32)
    expand[np.arange(s), np.arange(s) // SEL_LEN] = 1.0
    expand = jnp.asarray(expand, _BF16)

    h = x.reshape(m, D_MODEL)
    for l in range(depth):
        main, gates = _inproj(h, attn_g, w_main, w_gate, l)
        main4 = main.reshape(N_CHUNK, b, s, CHUNK_W)
        gates3 = gates.reshape(b, s, N_GATE)

        qa, ka = _fox_prep(gates3, f_bias_rows, l)
        fox_o = _fox_attn(main4, qa, ka)

        lam_init = 0.8 - 0.6 * math.exp(-0.3 * l)
        diff_o = _diff_attn(main4, diff_bias, lam_rows, subln_g, l, lam_init)

        kcmp, vcmp = _nsa_compress(main4, pos_flat, wk1, wk2, wv1, wv2, l, n_cmp)
        nsa_o = _nsa_attn(main4, kcmp, vcmp, gates3, nsa_bias, cmp_bias, ovt, expand)

        h = _outproj(fox_o.reshape(m, -1), diff_o.reshape(m, -1), nsa_o.reshape(m, -1), w_out_b, h, l)
        h = _ffn(h, ffn_g, w_up_b, ffn_conv_w, conv_b, w_down_b, final_norm_g[None], l, s,
                 final_norm=(l == depth - 1))
    return h.reshape(b, s, D_MODEL)
```
